```python
import math
import jax, jax.numpy as jnp
from jax import lax
import numpy as np

D_MODEL = 2048
BATCH = 4
SEQ = 2048
DEPTH = 4

MIX_WIDTH = D_MODEL
SSM_WIDTH = D_MODEL // 2
CONV_WIDTH = MIX_WIDTH - SSM_WIDTH
SSM_CH_PER_GROUP = 16
SSM_GROUPS = SSM_WIDTH // SSM_CH_PER_GROUP
SSM_STATE = 64
IN_WIDTH = SSM_WIDTH + 2 * CONV_WIDTH
CONV_K = 31
N_EXPERT_GROUPS = 4
EXPERTS_PER_GROUP = 8
N_EXPERTS = N_EXPERT_GROUPS * EXPERTS_PER_GROUP
EXPERT_TOP_K = 2
EXPERT_HIDDEN = D_MODEL // 8
PLE_DIM = 256
DEEPNORM_ALPHA = (2 * DEPTH) ** 0.25
DEEPNORM_BETA = (8 * DEPTH) ** -0.25
LN_EPS = 1e-5
LAMBDA_RE_MAX = -1e-4
DT_MIN = 1e-3
DT_MAX = 1e-1

kernel_name = 'hymba_s5_conformer_hmoe_deepnorm'


def layer_norm(x, g, b):
    xf = x.astype(jnp.float32)
    mu = jnp.mean(xf, axis=-1, keepdims=True)
    xc = xf - mu
    var = jnp.mean(xc * xc, axis=-1, keepdims=True)
    y = xc * lax.rsqrt(var + LN_EPS) * g.astype(jnp.float32) + b.astype(jnp.float32)
    return y.astype(x.dtype)


def s5_group(u, lam_re, lam_im, log_dt, b_re, b_im, c_re, c_im, d_skip, w_glu, b_glu):
    bsz, seqlen, _ = u.shape
    f32 = jnp.float32
    uf = u.astype(f32).reshape(bsz, seqlen, SSM_GROUPS, SSM_CH_PER_GROUP)
    lam = lax.complex(jnp.minimum(lam_re.astype(f32), LAMBDA_RE_MAX), lam_im.astype(f32))
    dt = jnp.exp(log_dt.astype(f32))[:, None]
    lam_bar = jnp.exp(lam * dt)
    b_c = lax.complex(b_re.astype(f32), b_im.astype(f32))
    b_bar = ((lam_bar - 1.0) / lam)[..., None] * b_c
    bu = jnp.einsum('blgh,gph->blgp', uf.astype(jnp.complex64), b_bar)
    a = jnp.broadcast_to(lam_bar[None, None], (1, seqlen, SSM_GROUPS, SSM_STATE))

    def combine(left, right):
        a1, s1 = left
        a2, s2 = right
        return a1 * a2, a2 * s1 + s2

    _, states = lax.associative_scan(combine, (a, bu), axis=1)
    c_c = lax.complex(c_re.astype(f32), c_im.astype(f32))
    y = jnp.einsum('blgp,ghp->blgh', states, c_c).real
    y = y + d_skip.astype(f32).reshape(SSM_GROUPS, SSM_CH_PER_GROUP) * uf
    y = y.reshape(bsz, seqlen, SSM_WIDTH)
    z = jax.nn.gelu(y)
    out = z * jax.nn.sigmoid(z @ w_glu.astype(f32) + b_glu.astype(f32))
    return out.astype(u.dtype)


def conformer_conv_group(v, g, w_dw, b_dw, ln_g, ln_b):
    h = v * jax.nn.sigmoid(g)
    h = lax.conv_general_dilated(h, w_dw[:, None, :], window_strides=(1,),
                                 padding=[(CONV_K - 1, 0)],
                                 dimension_numbers=('NWC', 'WIO', 'NWC'),
                                 feature_group_count=CONV_WIDTH) + b_dw
    h = layer_norm(h, ln_g, ln_b)
    return jax.nn.silu(h)


def hierarchical_moe(h, w_rg, b_rg, w_re, b_re, w_gate, w_up, w_down):
    bsz, seqlen, dm = h.shape
    n_tok = bsz * seqlen
    t = h.reshape(n_tok, dm)
    g_prob = jax.nn.softmax((t @ w_rg + b_rg).astype(jnp.float32), axis=-1)
    g_p, g_idx = lax.top_k(g_prob, 1)
    e_logits = (t @ w_re + b_re).astype(jnp.float32).reshape(n_tok, N_EXPERT_GROUPS, EXPERTS_PER_GROUP)
    sel = jnp.broadcast_to(g_idx[:, :, None], (n_tok, 1, EXPERTS_PER_GROUP))
    e_sel = jnp.take_along_axis(e_logits, sel, axis=1)[:, 0]
    e_val, e_idx = lax.top_k(e_sel, EXPERT_TOP_K)
    e_w = jax.nn.softmax(e_val, axis=-1) * g_p
    expert_id = g_idx * EXPERTS_PER_GROUP + e_idx
    combine_w = jnp.sum(jax.nn.one_hot(expert_id, N_EXPERTS, dtype=jnp.float32) * e_w[..., None], axis=1)
    gate = jnp.einsum('nd,edf->nef', t, w_gate)
    up = jnp.einsum('nd,edf->nef', t, w_up)
    act = jax.nn.silu(gate) * up * combine_w[..., None].astype(t.dtype)
    out = jnp.einsum('nef,efd->nd', act, w_down)
    return out.reshape(bsz, seqlen, dm)


def setup_inputs(seed: int = 0) -> dict:
    key = jax.random.key(seed)
    ks = jax.random.split(key, 36)
    f32 = jnp.float32

    def nrm(k, shape, scale):
        return scale * jax.random.normal(k, shape, f32)

    L, G, P, CH = DEPTH, SSM_GROUPS, SSM_STATE, SSM_CH_PER_GROUP
    return {
        'x': nrm(ks[0], (BATCH, SEQ, D_MODEL), 1.0),
        'p': nrm(ks[1], (DEPTH, BATCH, SEQ, PLE_DIM), 1.0),
        'w_in': nrm(ks[2], (L, D_MODEL, IN_WIDTH), D_MODEL ** -0.5),
        'b_in': nrm(ks[3], (L, IN_WIDTH), 0.01),
        'lam_re': -0.5 + nrm(ks[4], (L, G, P), 0.01),
        'lam_im': jnp.pi * jnp.arange(P, dtype=f32)[None, None, :] + nrm(ks[5], (L, G, P), 0.01),
        'log_dt': jax.random.uniform(ks[6], (L, G), f32, math.log(DT_MIN), math.log(DT_MAX)),
        'ssm_b_re': nrm(ks[7], (L, G, P, CH), (2 * CH) ** -0.5),
        'ssm_b_im': nrm(ks[8], (L, G, P, CH), (2 * CH) ** -0.5),
        'ssm_c_re': nrm(ks[9], (L, G, CH, P), (2 * P) ** -0.5),
        'ssm_c_im': nrm(ks[10], (L, G, CH, P), (2 * P) ** -0.5),
        'ssm_d': nrm(ks[11], (L, SSM_WIDTH), 1.0),
        'w_glu': nrm(ks[12], (L, SSM_WIDTH, SSM_WIDTH), SSM_WIDTH ** -0.5),
        'b_glu': nrm(ks[13], (L, SSM_WIDTH), 0.01),
        'w_dw': nrm(ks[14], (L, CONV_K, CONV_WIDTH), CONV_K ** -0.5),
        'b_dw': nrm(ks[15], (L, CONV_WIDTH), 0.01),
        'conv_ln_g': 1.0 + nrm(ks[16], (L, CONV_WIDTH), 0.01),
        'conv_ln_b': nrm(ks[17], (L, CONV_WIDTH), 0.01),
        'w_o': nrm(ks[18], (L, MIX_WIDTH, D_MODEL), DEEPNORM_BETA * MIX_WIDTH ** -0.5),
        'b_o': nrm(ks[19], (L, D_MODEL), 0.01),
        'ln1_g': 1.0 + nrm(ks[20], (L, D_MODEL), 0.01),
        'ln1_b': nrm(ks[21], (L, D_MODEL), 0.01),
        'w_rg': nrm(ks[22], (L, D_MODEL, N_EXPERT_GROUPS), D_MODEL ** -0.5),
        'b_rg': nrm(ks[23], (L, N_EXPERT_GROUPS), 0.01),
        'w_re': nrm(ks[24], (L, D_MODEL, N_EXPERTS), D_MODEL ** -0.5),
        'b_re': nrm(ks[25], (L, N_EXPERTS), 0.01),
        'w_gate': nrm(ks[26], (L, N_EXPERTS, D_MODEL, EXPERT_HIDDEN), D_MODEL ** -0.5),
        'w_up': nrm(ks[27], (L, N_EXPERTS, D_MODEL, EXPERT_HIDDEN), D_MODEL ** -0.5),
        'w_down': nrm(ks[28], (L, N_EXPERTS, EXPERT_HIDDEN, D_MODEL), DEEPNORM_BETA * EXPERT_HIDDEN ** -0.5),
        'ln2_g': 1.0 + nrm(ks[29], (L, D_MODEL), 0.01),
        'ln2_b': nrm(ks[30], (L, D_MODEL), 0.01),
        'w_p': nrm(ks[31], (L, PLE_DIM, D_MODEL), DEEPNORM_BETA * PLE_DIM ** -0.5),
        'w_pg': nrm(ks[32], (L, D_MODEL, D_MODEL), D_MODEL ** -0.5),
        'b_pg': nrm(ks[33], (L, D_MODEL), 0.01),
        'ln3_g': 1.0 + nrm(ks[34], (L, D_MODEL), 0.01),
        'ln3_b': nrm(ks[35], (L, D_MODEL), 0.01),
    }


def reference(x, p, w_in, b_in, lam_re, lam_im, log_dt, ssm_b_re, ssm_b_im, ssm_c_re, ssm_c_im,
              ssm_d, w_glu, b_glu, w_dw, b_dw, conv_ln_g, conv_ln_b, w_o, b_o, ln1_g, ln1_b,
              w_rg, b_rg, w_re, b_re, w_gate, w_up, w_down, ln2_g, ln2_b, w_p, w_pg, b_pg,
              ln3_g, ln3_b):
    for i in range(DEPTH):
        z = x @ w_in[i] + b_in[i]
        u_ssm = z[..., :SSM_WIDTH]
        v_conv = z[..., SSM_WIDTH:SSM_WIDTH + CONV_WIDTH]
        g_conv = z[..., SSM_WIDTH + CONV_WIDTH:]
        y_ssm = s5_group(u_ssm, lam_re[i], lam_im[i], log_dt[i], ssm_b_re[i], ssm_b_im[i],
                         ssm_c_re[i], ssm_c_im[i], ssm_d[i], w_glu[i], b_glu[i])
        y_conv = conformer_conv_group(v_conv, g_conv, w_dw[i], b_dw[i], conv_ln_g[i], conv_ln_b[i])
        y = jnp.concatenate([y_ssm, y_conv], axis=-1) @ w_o[i] + b_o[i]
        h = layer_norm(DEEPNORM_ALPHA * x + y, ln1_g[i], ln1_b[i])
        m = hierarchical_moe(h, w_rg[i], b_rg[i], w_re[i], b_re[i], w_gate[i], w_up[i], w_down[i])
        h = layer_norm(DEEPNORM_ALPHA * h + m, ln2_g[i], ln2_b[i])
        e = (p[i] @ w_p[i]) * jax.nn.sigmoid(h @ w_pg[i] + b_pg[i])
        x = layer_norm(DEEPNORM_ALPHA * h + e, ln3_g[i], ln3_b[i])
    return x
```

```python
import functools
import math

import jax
import jax.numpy as jnp
from jax import lax
from jax.experimental import pallas as pl
from jax.experimental.pallas import tpu as pltpu

BF16 = jnp.bfloat16
F32 = jnp.float32

CH_PER_GROUP = 16
EXPERT_TOP_K = 2
LN_EPS = 1e-5
LAMBDA_RE_MAX = -1e-4

LANES = 128
SUBLANES = 8
V7X_VMEM_BYTES = 64 * 1024 * 1024
VMEM_LIMIT_BYTES = V7X_VMEM_BYTES - 8 * 1024 * 1024

GROUPS_PER_SET = 8
SETS_PER_STEP = 2


def _params(semantics):
    return pltpu.CompilerParams(dimension_semantics=semantics, vmem_limit_bytes=VMEM_LIMIT_BYTES)


def _const_spec(shape):
    nd = len(shape)
    return pl.BlockSpec(shape, lambda *_: (0,) * nd)


def _layer_norm(v, g, b):
    mu = jnp.mean(v, axis=-1, keepdims=True)
    vc = v - mu
    var = jnp.mean(vc * vc, axis=-1, keepdims=True)
    return vc * lax.rsqrt(var + LN_EPS) * g + b


def _dot(a, b):
    return jnp.dot(a, b, preferred_element_type=F32)


def _inproj_kernel(x_ref, w_ref, b_ref, u_ref, hc_ref, *, ssm_w, conv_w):
    xb = x_ref[...].astype(BF16)

    def proj(lo, width):
        return _dot(xb, w_ref[:, lo:lo + width]) + b_ref[:, lo:lo + width]

    u_ref[...] = proj(0, ssm_w)
    v = proj(ssm_w, conv_w)
    g = proj(ssm_w + conv_w, conv_w)
    hc_ref[...] = v * jax.nn.sigmoid(g)


def _inproj(x2d, w_bf, b, *, ssm_w, conv_w, tm):
    n, d = x2d.shape
    in_w = w_bf.shape[1]
    return pl.pallas_call(
        functools.partial(_inproj_kernel, ssm_w=ssm_w, conv_w=conv_w),
        grid=(n // tm,),
        in_specs=[pl.BlockSpec((tm, d), lambda i: (i, 0)), _const_spec((d, in_w)), _const_spec((1, in_w))],
        out_specs=[pl.BlockSpec((tm, ssm_w), lambda i: (i, 0)), pl.BlockSpec((tm, conv_w), lambda i: (i, 0))],
        out_shape=[jax.ShapeDtypeStruct((n, ssm_w), F32), jax.ShapeDtypeStruct((n, conv_w), F32)],
        compiler_params=_params(("parallel",)),
    )(x2d, w_bf, b)


def _gelu_tanh(y):
    c = math.sqrt(2.0 / math.pi)
    return y * (0.5 * (1.0 + jnp.tanh(c * (y + 0.044715 * (y * y * y)))))


def _s5_kernel(u_ref, bm_ref, cm_ref, ar_ref, ai_ref, d_ref, z_ref, s_ref, xr_ref, xi_ref, *, tc, row_stride, unroll):
    nb = u_ref.shape[0]
    set_ch = GROUPS_PER_SET * CH_PER_GROUP
    n_lt = s_ref.shape[0]
    n_c = n_lt // 2
    rows = SETS_PER_STEP * nb
    lt = lambda j: slice(j * LANES, (j + 1) * LANES)

    @pl.when(pl.program_id(1) == 0)
    def _():
        xr_ref[...] = jnp.zeros_like(xr_ref)
        xi_ref[...] = jnp.zeros_like(xi_ref)

    for s in range(SETS_PER_STEP):
        for b in range(nb):
            ub = u_ref[b, :, s * set_ch:(s + 1) * set_ch].astype(BF16)
            r0 = (s * nb + b) * row_stride
            bu = _dot(ub, bm_ref[s])
            for j in range(n_lt):
                s_ref[j, r0:r0 + tc, :] = bu[:, lt(j)]

    ar = [ar_ref[:, lt(j)] for j in range(n_c)]
    ai = [ai_ref[:, lt(j)] for j in range(n_c)]

    def step(t, carry):
        rows_t = pl.ds(t, rows, stride=row_stride)
        out = []
        for j in range(n_c):
            xr, xi = carry[j]
            nxr = ar[j] * xr - ai[j] * xi + s_ref[j, rows_t, :]
            nxi = ar[j] * xi + ai[j] * xr + s_ref[n_c + j, rows_t, :]
            s_ref[j, rows_t, :] = nxr
            s_ref[n_c + j, rows_t, :] = nxi
            out.append((nxr, nxi))
        return tuple(out)

    def block(i, carry):
        for j in range(unroll):
            carry = step(i * unroll + j, carry)
        return carry

    init = tuple((xr_ref[:, lt(j)], xi_ref[:, lt(j)]) for j in range(n_c))
    final = lax.fori_loop(0, tc // unroll, block, init)
    for j in range(n_c):
        xr_ref[:, lt(j)] = final[j][0]
        xi_ref[:, lt(j)] = final[j][1]

    for s in range(SETS_PER_STEP):
        for b in range(nb):
            r0 = (s * nb + b) * row_stride
            u = u_ref[b, :, s * set_ch:(s + 1) * set_ch]
            st = jnp.concatenate([s_ref[j, r0:r0 + tc, :].astype(BF16) for j in range(n_lt)], axis=-1)
            y = _dot(st, cm_ref[s]) + d_ref[:, s * set_ch:(s + 1) * set_ch] * u
            z_ref[b, :, s * set_ch:(s + 1) * set_ch] = _gelu_tanh(y)


def _s5(u3d, bm, cm, ar, ai, d, *, tc):
    nb, seq, ssm_w = u3d.shape
    step_ch = SETS_PER_STEP * GROUPS_PER_SET * CH_PER_GROUP
    n_gb = ssm_w // step_ch
    state_w = bm.shape[-1]
    assert SETS_PER_STEP * nb == SUBLANES, "the scan fills the 8 sublanes with (set, batch)"
    row_stride = tc + SUBLANES
    return pl.pallas_call(
        functools.partial(_s5_kernel, tc=tc, row_stride=row_stride, unroll=8),
        grid=(n_gb, seq // tc),
        in_specs=[
            pl.BlockSpec((nb, tc, step_ch), lambda g, t: (0, t, g)),
            pl.BlockSpec((SETS_PER_STEP, bm.shape[1], state_w), lambda g, t: (g, 0, 0)),
            pl.BlockSpec((SETS_PER_STEP, state_w, cm.shape[2]), lambda g, t: (g, 0, 0)),
            pl.BlockSpec((None, SUBLANES, state_w // 2), lambda g, t: (g, 0, 0)),
            pl.BlockSpec((None, SUBLANES, state_w // 2), lambda g, t: (g, 0, 0)),
            pl.BlockSpec((1, step_ch), lambda g, t: (0, g)),
        ],
        out_specs=pl.BlockSpec((nb, tc, step_ch), lambda g, t: (0, t, g)),
        out_shape=jax.ShapeDtypeStruct((nb, seq, ssm_w), F32),
        scratch_shapes=[
            pltpu.VMEM((state_w // LANES, SUBLANES * row_stride, LANES), F32),
            pltpu.VMEM((SUBLANES, state_w // 2), F32),
            pltpu.VMEM((SUBLANES, state_w // 2), F32),
        ],
        compiler_params=_params(("parallel", "arbitrary")),
    )(u3d, bm, cm, ar, ai, d)


def _s5_params(lam_re, lam_im, log_dt, b_re, b_im, c_re, c_im, nb):
    g, p = lam_re.shape
    lam = lax.complex(jnp.minimum(lam_re, LAMBDA_RE_MAX), lam_im)
    dt = jnp.exp(log_dt)[:, None]
    lam_bar = jnp.exp(lam * dt)
    b_bar = ((lam_bar - 1.0) / lam)[..., None] * lax.complex(b_re, b_im)
    n_set = g // GROUPS_PER_SET
    eye = jnp.eye(GROUPS_PER_SET, dtype=F32)

    def in_mat(part):
        blk = part.reshape(n_set, GROUPS_PER_SET, p, CH_PER_GROUP)
        m = jnp.einsum('sgph,gk->sghkp', blk, eye)
        return m.reshape(n_set, GROUPS_PER_SET * CH_PER_GROUP, GROUPS_PER_SET * p)

    def out_mat(part):
        blk = part.reshape(n_set, GROUPS_PER_SET, CH_PER_GROUP, p)
        m = jnp.einsum('sghp,gk->sgpkh', blk, eye)
        return m.reshape(n_set, GROUPS_PER_SET * p, GROUPS_PER_SET * CH_PER_GROUP)

    bm = jnp.concatenate([in_mat(jnp.real(b_bar)), in_mat(jnp.imag(b_bar))], axis=-1).astype(BF16)
    cm = jnp.concatenate([out_mat(c_re), out_mat(-c_im)], axis=1).astype(BF16)

    def lam_rows(part):
        v = part.reshape(n_set // SETS_PER_STEP, SETS_PER_STEP, 1, GROUPS_PER_SET * p)
        v = jnp.broadcast_to(v, (n_set // SETS_PER_STEP, SETS_PER_STEP, nb, GROUPS_PER_SET * p))
        return v.reshape(n_set // SETS_PER_STEP, SETS_PER_STEP * nb, GROUPS_PER_SET * p)

    return bm, cm, lam_rows(jnp.real(lam_bar)), lam_rows(jnp.imag(lam_bar))


CONV_HALO = 32
CONV_ROWS = 32
CONV_LANES = 256


def _conv_kernel(cur_ref, halo_ref, w_ref, b_ref, g_ref, beta_ref, o_ref, buf_ref, *, tt, taps):
    halo = halo_ref[0]
    buf_ref[0:CONV_HALO, :] = jnp.where(pl.program_id(1) == 0, jnp.zeros_like(halo), halo)
    buf_ref[CONV_HALO:CONV_HALO + tt, :] = cur_ref[0]
    width = cur_ref.shape[-1]
    for c0 in range(0, width, CONV_LANES):
        cs = slice(c0, c0 + CONV_LANES)
        for r0 in range(0, tt, CONV_ROWS):
            acc = jnp.broadcast_to(b_ref[:, cs], (CONV_ROWS, CONV_LANES))
            for k in range(taps):
                off = CONV_HALO - (taps - 1) + k + r0
                acc = acc + buf_ref[off:off + CONV_ROWS, cs] * w_ref[k:k + 1, cs]
            o_ref[0, r0:r0 + CONV_ROWS, cs] = acc
    h = _layer_norm(o_ref[0], g_ref[...], beta_ref[...])
    o_ref[0] = h * jax.nn.sigmoid(h)


def _conv(hc3d, w_dw, b_dw, ln_g, ln_b, *, tt):
    nb, seq, width = hc3d.shape
    taps = w_dw.shape[0]
    assert taps - 1 <= CONV_HALO and tt % CONV_HALO == 0
    per = tt // CONV_HALO
    return pl.pallas_call(
        functools.partial(_conv_kernel, tt=tt, taps=taps),
        grid=(nb, seq // tt),
        in_specs=[
            pl.BlockSpec((1, tt, width), lambda b, t: (b, t, 0)),
            pl.BlockSpec((1, CONV_HALO, width), lambda b, t: (b, jnp.maximum(t * per - 1, 0), 0)),
            _const_spec((taps, width)), _const_spec((1, width)), _const_spec((1, width)), _const_spec((1, width)),
        ],
        out_specs=pl.BlockSpec((1, tt, width), lambda b, t: (b, t, 0)),
        out_shape=jax.ShapeDtypeStruct((nb, seq, width), F32),
        scratch_shapes=[pltpu.VMEM((CONV_HALO + tt, width), F32)],
        compiler_params=_params(("parallel", "parallel")),
    )(hc3d, hc3d, w_dw, b_dw, ln_g, ln_b)


def _first_lane_of_max(vals, lane):
    m = jnp.max(vals, axis=-1, keepdims=True)
    idx = jnp.min(jnp.where(vals == m, lane, LANES), axis=-1, keepdims=True)
    return m, idx


def _mix_out_kernel(z_ref, yc_ref, x_ref, wglu_ref, bglu_ref, wo_ref, bo_ref, g1_ref, b1_ref,
                    wrh_ref, wrl_ref, br_ref, h_ref, ri_ref, rw_ref, cnt_ref, carry_ref,
                    *, alpha, ssm_w, n_groups, per_group):
    tm = z_ref.shape[0]

    @pl.when(pl.program_id(0) == 0)
    def _():
        carry_ref[...] = jnp.zeros_like(carry_ref)

    z = z_ref[...]
    zs = z * jax.nn.sigmoid(_dot(z.astype(BF16), wglu_ref[...]) + bglu_ref[...])
    y = (_dot(zs.astype(BF16), wo_ref[0:ssm_w, :]) + _dot(yc_ref[...].astype(BF16), wo_ref[ssm_w:, :])
         + bo_ref[...])
    h = _layer_norm(alpha * x_ref[...] + y, g1_ref[...], b1_ref[...])
    h_ref[...] = h

    hh = h.astype(BF16)
    hl = (h - hh.astype(F32)).astype(BF16)
    logits = _dot(hh, wrh_ref[...]) + (_dot(hh, wrl_ref[...]) + _dot(hl, wrh_ref[...])) + br_ref[...]

    lane = lax.broadcasted_iota(jnp.int32, (tm, LANES), 1)
    neg = -jnp.inf
    gmask = lane < n_groups
    gmax, gidx = _first_lane_of_max(jnp.where(gmask, logits, neg), lane)
    gsum = jnp.sum(jnp.where(gmask, jnp.exp(logits - gmax), 0.0), axis=-1, keepdims=True)
    g_p = 1.0 / gsum
    elo = n_groups + gidx * per_group
    le = jnp.where((lane >= elo) & (lane < elo + per_group), logits, neg)
    v1, i1 = _first_lane_of_max(le, lane)
    v2, i2 = _first_lane_of_max(jnp.where(lane == i1, neg, le), lane)
    e2 = jnp.exp(v2 - v1)
    w1 = g_p / (1.0 + e2)
    w2 = w1 * e2

    oh1 = lane == i1
    oh2 = lane == i2
    oh = jnp.where(oh1 | oh2, 1.0, 0.0)
    row = lax.broadcasted_iota(jnp.int32, (tm, tm), 0)
    col = lax.broadcasted_iota(jnp.int32, (tm, tm), 1)
    earlier = jnp.where(col < row, 1.0, 0.0).astype(BF16)
    base = _dot(earlier, oh.astype(BF16)) + carry_ref[...]
    r1 = jnp.sum(jnp.where(oh1, base, 0.0), axis=-1, keepdims=True).astype(jnp.int32)
    r2 = jnp.sum(jnp.where(oh2, base, 0.0), axis=-1, keepdims=True).astype(jnp.int32)
    carry_ref[...] += jnp.sum(oh, axis=0, keepdims=True)
    cnt_ref[...] = carry_ref[...]

    ri_ref[...] = jnp.where(lane == 0, i1 - n_groups,
                            jnp.where(lane == 1, i2 - n_groups,
                                      jnp.where(lane == 2, r1, jnp.where(lane == 3, r2, 0))))
    rw_ref[...] = jnp.where(lane == 0, w1, jnp.where(lane == 1, w2, 0.0))


def _mix_out(z, yc, x2d, wglu, bglu, wo, bo, g1, b1, wrh, wrl, br, *, alpha, n_groups, per_group, tm):
    n, d = x2d.shape
    ssm_w = z.shape[1]
    conv_w = yc.shape[1]
    row = lambda w: pl.BlockSpec((tm, w), lambda i: (i, 0))
    return pl.pallas_call(
        functools.partial(_mix_out_kernel, alpha=alpha, ssm_w=ssm_w, n_groups=n_groups, per_group=per_group),
        grid=(n // tm,),
        in_specs=[row(ssm_w), row(conv_w), row(d),
                  _const_spec(wglu.shape), _const_spec(bglu.shape), _const_spec(wo.shape), _const_spec(bo.shape),
                  _const_spec(g1.shape), _const_spec(b1.shape),
                  _const_spec(wrh.shape), _const_spec(wrl.shape), _const_spec(br.shape)],
        out_specs=[row(d), row(LANES), row(LANES), _const_spec((1, LANES))],
        out_shape=[jax.ShapeDtypeStruct((n, d), F32), jax.ShapeDtypeStruct((n, LANES), jnp.int32),
                   jax.ShapeDtypeStruct((n, LANES), F32), jax.ShapeDtypeStruct((1, LANES), F32)],
        scratch_shapes=[pltpu.VMEM((1, LANES), F32)],
        compiler_params=_params(("arbitrary",)),
    )(z, yc, x2d, wglu, bglu, wo, bo, g1, b1, wrh, wrl, br)


def _row_gather_start(idx_ref, base, src_hbm, dst, sem, n_rows, unroll=8):
    def body(i, c):
        for j in range(unroll):
            r = i * unroll + j
            pltpu.make_async_copy(src_hbm.at[pl.ds(idx_ref[base + r], 1), :], dst.at[pl.ds(r, 1), :], sem).start()
        return c
    lax.fori_loop(0, n_rows // unroll, body, 0)


def _row_gather_wait(src_hbm, dst, sem, n_rows):
    pltpu.make_async_copy(src_hbm.at[pl.ds(0, n_rows), :], dst, sem).wait()


def _moe_kernel(te_ref, na_ref, src_ref, h_hbm, wg_ref, wu_ref, wd_ref, o_ref, buf, sem, wgb, wub, wdb, *, tm):
    t = pl.program_id(0)
    na = na_ref[0]
    slot = lax.rem(t, 2)

    @pl.when((t == 0) & (na > 0))
    def _():
        _row_gather_start(src_ref, 0, h_hbm, buf.at[0], sem.at[0], tm)

    @pl.when(t < na)
    def _():
        _row_gather_wait(h_hbm, buf.at[slot], sem.at[slot], tm)

    @pl.when(t + 1 < na)
    def _():
        _row_gather_start(src_ref, (t + 1) * tm, h_hbm, buf.at[1 - slot], sem.at[1 - slot], tm)

    new_expert = (t == 0) | (te_ref[t] != te_ref[jnp.maximum(t - 1, 0)])

    @pl.when(new_expert & (t < na))
    def _():
        wgb[...] = wg_ref[0].astype(BF16)
        wub[...] = wu_ref[0].astype(BF16)
        wdb[...] = wd_ref[0].astype(BF16)

    @pl.when(t < na)
    def _():
        xb = buf[slot].astype(BF16)
        g = _dot(xb, wgb[...])
        u = _dot(xb, wub[...])
        a = (g * jax.nn.sigmoid(g)) * u
        o_ref[...] = _dot(a.astype(BF16), wdb[...])

    @pl.when(t >= na)
    def _():
        o_ref[...] = jnp.zeros_like(o_ref)


def _moe(tile_e, n_active, src, h, w_gate, w_up, w_down, *, tm):
    n, d = h.shape
    n_exp, _, f = w_gate.shape
    n_tiles = tile_e.shape[0]
    return pl.pallas_call(
        functools.partial(_moe_kernel, tm=tm),
        grid_spec=pltpu.PrefetchScalarGridSpec(
            num_scalar_prefetch=3,
            grid=(n_tiles,),
            in_specs=[
                pl.BlockSpec(memory_space=pl.ANY),
                pl.BlockSpec((1, d, f), lambda t, te, na, src: (te[t], 0, 0)),
                pl.BlockSpec((1, d, f), lambda t, te, na, src: (te[t], 0, 0)),
                pl.BlockSpec((1, f, d), lambda t, te, na, src: (te[t], 0, 0)),
            ],
            out_specs=pl.BlockSpec((tm, d), lambda t, te, na, src: (t, 0)),
            scratch_shapes=[
                pltpu.VMEM((2, tm, d), F32), pltpu.SemaphoreType.DMA((2,)),
                pltpu.VMEM((d, f), BF16), pltpu.VMEM((d, f), BF16), pltpu.VMEM((f, d), BF16),
            ],
        ),
        out_shape=jax.ShapeDtypeStruct((n_tiles * tm, d), F32),
        compiler_params=_params(("arbitrary",)),
    )(tile_e, n_active, src, h, w_gate, w_up, w_down)


def _dispatch_plan(ri, cnt, *, n_groups, n_exp, tm):
    n = ri.shape[0]
    eid = ri[:, 0:EXPERT_TOP_K]
    rank = ri[:, EXPERT_TOP_K:2 * EXPERT_TOP_K]
    counts = cnt[0, n_groups:n_groups + n_exp].astype(jnp.int32)
    tiles_per = (counts + tm - 1) // tm
    tile_end = jnp.cumsum(tiles_per)
    n_active = tile_end[-1]
    offsets = (tile_end - tiles_per) * tm
    dest = offsets[eid] + rank
    n_tiles = (n * EXPERT_TOP_K) // tm + n_exp
    t_idx = jnp.minimum(jnp.arange(n_tiles, dtype=jnp.int32), n_active - 1)
    tile_e = jnp.minimum(jnp.searchsorted(tile_end, t_idx, side='right'), n_exp - 1).astype(jnp.int32)
    tok = jnp.broadcast_to(jnp.arange(n, dtype=jnp.int32)[:, None], dest.shape)
    src = jnp.zeros((n_tiles * tm,), jnp.int32).at[dest.reshape(-1)].set(tok.reshape(-1))
    return tile_e, n_active.reshape(1).astype(jnp.int32), src, dest


def _post_kernel(d0_ref, d1_ref, ys_hbm, h_ref, rw_ref, p_ref, wp_ref, wpg_ref, bpg_ref,
                 g2_ref, b2_ref, g3_ref, b3_ref, o_ref, buf, sem, *, alpha, tm):
    i = pl.program_id(0)
    slot = lax.rem(i, 2)

    def start(tile, sl):
        _row_gather_start(d0_ref, tile * tm, ys_hbm, buf.at[sl, 0], sem.at[sl], tm)
        _row_gather_start(d1_ref, tile * tm, ys_hbm, buf.at[sl, 1], sem.at[sl], tm)

    @pl.when(i == 0)
    def _():
        start(0, 0)

    _row_gather_wait(ys_hbm, buf.at[slot, 0], sem.at[slot], tm)
    _row_gather_wait(ys_hbm, buf.at[slot, 1], sem.at[slot], tm)

    @pl.when(i + 1 < pl.num_programs(0))
    def _():
        start(i + 1, 1 - slot)

    rw = rw_ref[...]
    m = rw[:, 0:1] * buf[slot, 0] + rw[:, 1:2] * buf[slot, 1]
    h2 = _layer_norm(alpha * h_ref[...] + m, g2_ref[...], b2_ref[...])
    gate = jax.nn.sigmoid(_dot(h2.astype(BF16), wpg_ref[...]) + bpg_ref[...])
    e = _dot(p_ref[...].astype(BF16), wp_ref[...]) * gate
    o_ref[...] = _layer_norm(alpha * h2 + e, g3_ref[...], b3_ref[...])


def _post(d0, d1, ys, h, rw, p2d, wp, wpg, bpg, g2, b2, g3, b3, *, alpha, tm):
    n, d = h.shape
    ple = p2d.shape[1]
    c = lambda shape: pl.BlockSpec(shape, lambda i, a, b: (0,) * len(shape))
    row = lambda w: pl.BlockSpec((tm, w), lambda i, a, b: (i, 0))
    return pl.pallas_call(
        functools.partial(_post_kernel, alpha=alpha, tm=tm),
        grid_spec=pltpu.PrefetchScalarGridSpec(
            num_scalar_prefetch=2,
            grid=(n // tm,),
            in_specs=[pl.BlockSpec(memory_space=pl.ANY), row(d), row(LANES), row(ple),
                      c(wp.shape), c(wpg.shape), c(bpg.shape), c(g2.shape), c(b2.shape), c(g3.shape), c(b3.shape)],
            out_specs=row(d),
            scratch_shapes=[pltpu.VMEM((2, EXPERT_TOP_K, tm, d), F32), pltpu.SemaphoreType.DMA((2,))],
        ),
        out_shape=jax.ShapeDtypeStruct((n, d), F32),
        compiler_params=_params(("arbitrary",)),
    )(d0, d1, ys, h, rw, p2d, wp, wpg, bpg, g2, b2, g3, b3)


def _tiles(n, seq):
    return dict(
        inproj_tm=min(512, n),
        s5_tc=min(512, seq),
        conv_tt=min(128, seq),
        mix_tm=min(256, n),
        moe_tm=min(256, n),
        post_tm=min(256, n),
    )


def kernel(x, p, w_in, b_in, lam_re, lam_im, log_dt, ssm_b_re, ssm_b_im, ssm_c_re, ssm_c_im, ssm_d, w_glu, b_glu, w_dw, b_dw, conv_ln_g, conv_ln_b, w_o, b_o, ln1_g, ln1_b, w_rg, b_rg, w_re, b_re, w_gate, w_up, w_down, ln2_g, ln2_b, w_p, w_pg, b_pg, ln3_g, ln3_b):
    depth = w_in.shape[0]
    nb, seq, d = x.shape
    n = nb * seq
    ssm_w = w_glu.shape[1]
    conv_w = w_dw.shape[2]
    n_groups = w_rg.shape[2]
    n_exp = w_re.shape[2]
    per_group = n_exp // n_groups
    alpha = (2 * depth) ** 0.25
    tl = _tiles(n, seq)
    row2 = lambda v: v.reshape(1, -1)

    x2d = x.reshape(n, d)
    for i in range(depth):
        u, hc = _inproj(x2d, w_in[i].astype(BF16), row2(b_in[i]), ssm_w=ssm_w, conv_w=conv_w, tm=tl['inproj_tm'])

        bm, cm, ar, ai = _s5_params(lam_re[i], lam_im[i], log_dt[i], ssm_b_re[i], ssm_b_im[i],
                                    ssm_c_re[i], ssm_c_im[i], nb)
        z = _s5(u.reshape(nb, seq, ssm_w), bm, cm, ar, ai, row2(ssm_d[i]), tc=tl['s5_tc']).reshape(n, ssm_w)

        yc = _conv(hc.reshape(nb, seq, conv_w), w_dw[i], row2(b_dw[i]), row2(conv_ln_g[i]), row2(conv_ln_b[i]),
                   tt=tl['conv_tt']).reshape(n, conv_w)

        wr = jnp.concatenate([w_rg[i], w_re[i]], axis=1)
        wr = jnp.pad(wr, ((0, 0), (0, LANES - wr.shape[1])))
        wrh = wr.astype(BF16)
        wrl = (wr - wrh.astype(F32)).astype(BF16)
        br = jnp.pad(jnp.concatenate([b_rg[i], b_re[i]]), (0, LANES - n_groups - n_exp)).reshape(1, LANES)
        h, ri, rw, cnt = _mix_out(z, yc, x2d, w_glu[i].astype(BF16), row2(b_glu[i]), w_o[i].astype(BF16),
                                  row2(b_o[i]), row2(ln1_g[i]), row2(ln1_b[i]), wrh, wrl, br,
                                  alpha=alpha, n_groups=n_groups, per_group=per_group, tm=tl['mix_tm'])

        tile_e, n_active, src, dest = _dispatch_plan(ri, cnt, n_groups=n_groups, n_exp=n_exp, tm=tl['moe_tm'])
        ys = _moe(tile_e, n_active, src, h, w_gate[i], w_up[i], w_down[i], tm=tl['moe_tm'])

        x2d = _post(dest[:, 0], dest[:, 1], ys, h, rw, p[i].reshape(n, -1), w_p[i].astype(BF16),
                    w_pg[i].astype(BF16), row2(b_pg[i]), row2(ln2_g[i]), row2(ln2_b[i]),
                    row2(ln3_g[i]), row2(ln3_b[i]), alpha=alpha, tm=tl['post_tm'])
    return x2d.reshape(nb, seq, d)
```

```python
import functools
import math

import jax
import jax.numpy as jnp
from jax import lax
from jax.experimental import pallas as pl
from jax.experimental.pallas import tpu as pltpu

BF16 = jnp.bfloat16
F32 = jnp.float32

CH_PER_GROUP = 16
EXPERT_TOP_K = 2
LN_EPS = 1e-5
LAMBDA_RE_MAX = -1e-4

LANES = 128
SUBLANES = 8
V7X_VMEM_BYTES = 64 * 1024 * 1024
VMEM_LIMIT_BYTES = V7X_VMEM_BYTES - 8 * 1024 * 1024

GROUPS_PER_SET = 8
SETS_PER_STEP = 2


def _params(semantics):
    return pltpu.CompilerParams(dimension_semantics=semantics, vmem_limit_bytes=VMEM_LIMIT_BYTES)


def _const_spec(shape):
    nd = len(shape)
    return pl.BlockSpec(shape, lambda *_: (0,) * nd)


def _layer_norm(v, g, b):
    mu = jnp.mean(v, axis=-1, keepdims=True)
    vc = v - mu
    var = jnp.mean(vc * vc, axis=-1, keepdims=True)
    return vc * lax.rsqrt(var + LN_EPS) * g + b


def _dot(a, b):
    return jnp.dot(a, b, preferred_element_type=F32)


def _inproj_kernel(x_ref, w_ref, b_ref, u_ref, hc_ref, *, ssm_w, conv_w):
    xb = x_ref[...].astype(BF16)

    def proj(lo, width):
        return _dot(xb, w_ref[:, lo:lo + width]) + b_ref[:, lo:lo + width]

    u_ref[...] = proj(0, ssm_w)
    v = proj(ssm_w, conv_w)
    g = proj(ssm_w + conv_w, conv_w)
    hc_ref[...] = v * jax.nn.sigmoid(g)


def _inproj(x2d, w_bf, b, *, ssm_w, conv_w, tm):
    n, d = x2d.shape
    in_w = w_bf.shape[1]
    return pl.pallas_call(
        functools.partial(_inproj_kernel, ssm_w=ssm_w, conv_w=conv_w),
        grid=(n // tm,),
        in_specs=[pl.BlockSpec((tm, d), lambda i: (i, 0)), _const_spec((d, in_w)), _const_spec((1, in_w))],
        out_specs=[pl.BlockSpec((tm, ssm_w), lambda i: (i, 0)), pl.BlockSpec((tm, conv_w), lambda i: (i, 0))],
        out_shape=[jax.ShapeDtypeStruct((n, ssm_w), F32), jax.ShapeDtypeStruct((n, conv_w), F32)],
        compiler_params=_params(("parallel",)),
    )(x2d, w_bf, b)


def _gelu_tanh(y):
    c = math.sqrt(2.0 / math.pi)
    return y * (0.5 * (1.0 + jnp.tanh(c * (y + 0.044715 * (y * y * y)))))


def _s5_kernel(u_ref, bm_ref, cm_ref, ar_ref, ai_ref, d_ref, z_ref, s_ref, xr_ref, xi_ref, *, tc, row_stride, unroll):
    nb = u_ref.shape[0]
    set_ch = GROUPS_PER_SET * CH_PER_GROUP
    n_lt = s_ref.shape[0]
    n_c = n_lt // 2
    rows = SETS_PER_STEP * nb
    lt = lambda j: slice(j * LANES, (j + 1) * LANES)

    @pl.when(pl.program_id(1) == 0)
    def _():
        xr_ref[...] = jnp.zeros_like(xr_ref)
        xi_ref[...] = jnp.zeros_like(xi_ref)

    for s in range(SETS_PER_STEP):
        for b in range(nb):
            ub = u_ref[b, :, s * set_ch:(s + 1) * set_ch].astype(BF16)
            r0 = (s * nb + b) * row_stride
            bu = _dot(ub, bm_ref[s])
            for j in range(n_lt):
                s_ref[j, r0:r0 + tc, :] = bu[:, lt(j)]

    ar = [ar_ref[:, lt(j)] for j in range(n_c)]
    ai = [ai_ref[:, lt(j)] for j in range(n_c)]

    def step(t, carry):
        rows_t = pl.ds(t, rows, stride=row_stride)
        out = []
        for j in range(n_c):
            xr, xi = carry[j]
            nxr = ar[j] * xr - ai[j] * xi + s_ref[j, rows_t, :]
            nxi = ar[j] * xi + ai[j] * xr + s_ref[n_c + j, rows_t, :]
            s_ref[j, rows_t, :] = nxr
            s_ref[n_c + j, rows_t, :] = nxi
            out.append((nxr, nxi))
        return tuple(out)

    def block(i, carry):
        for j in range(unroll):
            carry = step(i * unroll + j, carry)
        return carry

    init = tuple((xr_ref[:, lt(j)], xi_ref[:, lt(j)]) for j in range(n_c))
    final = lax.fori_loop(0, tc // unroll, block, init)
    for j in range(n_c):
        xr_ref[:, lt(j)] = final[j][0]
        xi_ref[:, lt(j)] = final[j][1]

    for s in range(SETS_PER_STEP):
        for b in range(nb):
            r0 = (s * nb + b) * row_stride
            u = u_ref[b, :, s * set_ch:(s + 1) * set_ch]
            st = jnp.concatenate([s_ref[j, r0:r0 + tc, :].astype(BF16) for j in range(n_lt)], axis=-1)
            y = _dot(st, cm_ref[s]) + d_ref[:, s * set_ch:(s + 1) * set_ch] * u
            z_ref[b, :, s * set_ch:(s + 1) * set_ch] = _gelu_tanh(y)


def _s5(u3d, bm, cm, ar, ai, d, *, tc):
    nb, seq, ssm_w = u3d.shape
    step_ch = SETS_PER_STEP * GROUPS_PER_SET * CH_PER_GROUP
    n_gb = ssm_w // step_ch
    state_w = bm.shape[-1]
    assert SETS_PER_STEP * nb == SUBLANES, "the scan fills the 8 sublanes with (set, batch)"
    row_stride = tc + SUBLANES
    return pl.pallas_call(
        functools.partial(_s5_kernel, tc=tc, row_stride=row_stride, unroll=8),
        grid=(n_gb, seq // tc),
        in_specs=[
            pl.BlockSpec((nb, tc, step_ch), lambda g, t: (0, t, g)),
            pl.BlockSpec((SETS_PER_STEP, bm.shape[1], state_w), lambda g, t: (g, 0, 0)),
            pl.BlockSpec((SETS_PER_STEP, state_w, cm.shape[2]), lambda g, t: (g, 0, 0)),
            pl.BlockSpec((None, SUBLANES, state_w // 2), lambda g, t: (g, 0, 0)),
            pl.BlockSpec((None, SUBLANES, state_w // 2), lambda g, t: (g, 0, 0)),
            pl.BlockSpec((1, step_ch), lambda g, t: (0, g)),
        ],
        out_specs=pl.BlockSpec((nb, tc, step_ch), lambda g, t: (0, t, g)),
        out_shape=jax.ShapeDtypeStruct((nb, seq, ssm_w), F32),
        scratch_shapes=[
            pltpu.VMEM((state_w // LANES, SUBLANES * row_stride, LANES), F32),
            pltpu.VMEM((SUBLANES, state_w // 2), F32),
            pltpu.VMEM((SUBLANES, state_w // 2), F32),
        ],
        compiler_params=_params(("parallel", "arbitrary")),
    )(u3d, bm, cm, ar, ai, d)


def _s5_params(lam_re, lam_im, log_dt, b_re, b_im, c_re, c_im, nb):
    g, p = lam_re.shape
    lam = lax.complex(jnp.minimum(lam_re, LAMBDA_RE_MAX), lam_im)
    dt = jnp.exp(log_dt)[:, None]
    lam_bar = jnp.exp(lam * dt)
    b_bar = ((lam_bar - 1.0) / lam)[..., None] * lax.complex(b_re, b_im)
    n_set = g // GROUPS_PER_SET
    eye = jnp.eye(GROUPS_PER_SET, dtype=F32)

    def in_mat(part):
        blk = part.reshape(n_set, GROUPS_PER_SET, p, CH_PER_GROUP)
        m = jnp.einsum('sgph,gk->sghkp', blk, eye)
        return m.reshape(n_set, GROUPS_PER_SET * CH_PER_GROUP, GROUPS_PER_SET * p)

    def out_mat(part):
        blk = part.reshape(n_set, GROUPS_PER_SET, CH_PER_GROUP, p)
        m = jnp.einsum('sghp,gk->sgpkh', blk, eye)
        return m.reshape(n_set, GROUPS_PER_SET * p, GROUPS_PER_SET * CH_PER_GROUP)

    bm = jnp.concatenate([in_mat(jnp.real(b_bar)), in_mat(jnp.imag(b_bar))], axis=-1).astype(BF16)
    cm = jnp.concatenate([out_mat(c_re), out_mat(-c_im)], axis=1).astype(BF16)

    def lam_rows(part):
        v = part.reshape(n_set // SETS_PER_STEP, SETS_PER_STEP, 1, GROUPS_PER_SET * p)
        v = jnp.broadcast_to(v, (n_set // SETS_PER_STEP, SETS_PER_STEP, nb, GROUPS_PER_SET * p))
        return v.reshape(n_set // SETS_PER_STEP, SETS_PER_STEP * nb, GROUPS_PER_SET * p)

    return bm, cm, lam_rows(jnp.real(lam_bar)), lam_rows(jnp.imag(lam_bar))


CONV_HALO = 32
CONV_ROWS = 32
CONV_LANES = 256


def _conv_kernel(cur_ref, halo_ref, w_ref, b_ref, g_ref, beta_ref, o_ref, buf_ref, *, tt, taps):
    halo = halo_ref[0]
    buf_ref[0:CONV_HALO, :] = jnp.where(pl.program_id(1) == 0, jnp.zeros_like(halo), halo)
    buf_ref[CONV_HALO:CONV_HALO + tt, :] = cur_ref[0]
    width = cur_ref.shape[-1]
    for c0 in range(0, width, CONV_LANES):
        cs = slice(c0, c0 + CONV_LANES)
        for r0 in range(0, tt, CONV_ROWS):
            acc = jnp.broadcast_to(b_ref[:, cs], (CONV_ROWS, CONV_LANES))
            for k in range(taps):
                off = CONV_HALO - (taps - 1) + k + r0
                acc = acc + buf_ref[off:off + CONV_ROWS, cs] * w_ref[k:k + 1, cs]
            o_ref[0, r0:r0 + CONV_ROWS, cs] = acc
    h = _layer_norm(o_ref[0], g_ref[...], beta_ref[...])
    o_ref[0] = h * jax.nn.sigmoid(h)


def _conv(hc3d, w_dw, b_dw, ln_g, ln_b, *, tt):
    nb, seq, width = hc3d.shape
    taps = w_dw.shape[0]
    assert taps - 1 <= CONV_HALO and tt % CONV_HALO == 0
    per = tt // CONV_HALO
    return pl.pallas_call(
        functools.partial(_conv_kernel, tt=tt, taps=taps),
        grid=(nb, seq // tt),
        in_specs=[
            pl.BlockSpec((1, tt, width), lambda b, t: (b, t, 0)),
            pl.BlockSpec((1, CONV_HALO, width), lambda b, t: (b, jnp.maximum(t * per - 1, 0), 0)),
            _const_spec((taps, width)), _const_spec((1, width)), _const_spec((1, width)), _const_spec((1, width)),
        ],
        out_specs=pl.BlockSpec((1, tt, width), lambda b, t: (b, t, 0)),
        out_shape=jax.ShapeDtypeStruct((nb, seq, width), F32),
        scratch_shapes=[pltpu.VMEM((CONV_HALO + tt, width), F32)],
        compiler_params=_params(("parallel", "parallel")),
    )(hc3d, hc3d, w_dw, b_dw, ln_g, ln_b)


def _first_lane_of_max(vals, lane):
    m = jnp.max(vals, axis=-1, keepdims=True)
    idx = jnp.min(jnp.where(vals == m, lane, LANES), axis=-1, keepdims=True)
    return m, idx


def _pack_pairs(lo, hi):
    lo_bits = lax.bitcast_convert_type(lo.astype(BF16).astype(F32), jnp.uint32)
    hi_bits = lax.bitcast_convert_type(hi.astype(BF16).astype(F32), jnp.uint32)
    return hi_bits | (lo_bits >> 16)


def _unpack_pairs(words):
    lo = lax.bitcast_convert_type(words << 16, F32)
    hi = lax.bitcast_convert_type(words & jnp.uint32(0xFFFF0000), F32)
    return lo, hi


def _mix_out_kernel(z_ref, yc_ref, x_ref, wglu_ref, bglu_ref, wo_ref, bo_ref, g1_ref, b1_ref,
                    wrh_ref, wrl_ref, br_ref, h_ref, hp_ref, ri_ref, rw_ref, cnt_ref, carry_ref,
                    *, alpha, ssm_w, n_groups, per_group):
    tm = z_ref.shape[0]
    d = x_ref.shape[1]
    rpt = d // 2 // LANES
    i = pl.program_id(0)
    steps_half = pl.num_programs(0) // 2

    @pl.when(i == 0)
    def _():
        cnt_ref[...] = jnp.zeros_like(cnt_ref)

    @pl.when((i == 0) | (i == steps_half))
    def _():
        carry_ref[...] = jnp.zeros_like(carry_ref)

    z = z_ref[...]
    zs = z * jax.nn.sigmoid(_dot(z.astype(BF16), wglu_ref[...]) + bglu_ref[...])
    y = (_dot(zs.astype(BF16), wo_ref[0:ssm_w, :]) + _dot(yc_ref[...].astype(BF16), wo_ref[ssm_w:, :])
         + bo_ref[...])
    h = _layer_norm(alpha * x_ref[...] + y, g1_ref[...], b1_ref[...])
    h_ref[...] = h
    for c in range(rpt):
        lo = h[:, c * LANES:(c + 1) * LANES]
        hi = h[:, d // 2 + c * LANES:d // 2 + (c + 1) * LANES]
        hp_ref[pl.ds(c, tm, stride=rpt), :] = _pack_pairs(lo, hi)

    hh = h.astype(BF16)
    hl = (h - hh.astype(F32)).astype(BF16)
    logits = _dot(hh, wrh_ref[...]) + (_dot(hh, wrl_ref[...]) + _dot(hl, wrh_ref[...])) + br_ref[...]

    lane = lax.broadcasted_iota(jnp.int32, (tm, LANES), 1)
    neg = -jnp.inf
    gmask = lane < n_groups
    gmax, gidx = _first_lane_of_max(jnp.where(gmask, logits, neg), lane)
    gsum = jnp.sum(jnp.where(gmask, jnp.exp(logits - gmax), 0.0), axis=-1, keepdims=True)
    g_p = 1.0 / gsum
    elo = n_groups + gidx * per_group
    le = jnp.where((lane >= elo) & (lane < elo + per_group), logits, neg)
    v1, i1 = _first_lane_of_max(le, lane)
    v2, i2 = _first_lane_of_max(jnp.where(lane == i1, neg, le), lane)
    e2 = jnp.exp(v2 - v1)
    w1 = g_p / (1.0 + e2)
    w2 = w1 * e2

    oh1 = lane == i1
    oh2 = lane == i2
    oh = jnp.where(oh1 | oh2, 1.0, 0.0)
    row = lax.broadcasted_iota(jnp.int32, (tm, tm), 0)
    col = lax.broadcasted_iota(jnp.int32, (tm, tm), 1)
    earlier = jnp.where(col < row, 1.0, 0.0).astype(BF16)
    base = _dot(earlier, oh.astype(BF16)) + carry_ref[...]
    r1 = jnp.sum(jnp.where(oh1, base, 0.0), axis=-1, keepdims=True).astype(jnp.int32)
    r2 = jnp.sum(jnp.where(oh2, base, 0.0), axis=-1, keepdims=True).astype(jnp.int32)
    carry_ref[...] += jnp.sum(oh, axis=0, keepdims=True)

    @pl.when(i < steps_half)
    def _():
        cnt_ref[0:1, :] = carry_ref[...]

    @pl.when(i >= steps_half)
    def _():
        cnt_ref[1:2, :] = carry_ref[...]

    ri_ref[...] = jnp.where(lane == 0, i1 - n_groups,
                            jnp.where(lane == 1, i2 - n_groups,
                                      jnp.where(lane == 2, r1, jnp.where(lane == 3, r2, 0))))
    rw_ref[...] = jnp.where(lane == 0, w1, jnp.where(lane == 1, w2, 0.0))


def _mix_out(z, yc, x2d, wglu, bglu, wo, bo, g1, b1, wrh, wrl, br, *, alpha, n_groups, per_group, tm):
    n, d = x2d.shape
    ssm_w = z.shape[1]
    conv_w = yc.shape[1]
    rpt = d // 2 // LANES
    assert (n // tm) % 2 == 0
    row = lambda w: pl.BlockSpec((tm, w), lambda i: (i, 0))
    return pl.pallas_call(
        functools.partial(_mix_out_kernel, alpha=alpha, ssm_w=ssm_w, n_groups=n_groups, per_group=per_group),
        grid=(n // tm,),
        in_specs=[row(ssm_w), row(conv_w), row(d),
                  _const_spec(wglu.shape), _const_spec(bglu.shape), _const_spec(wo.shape), _const_spec(bo.shape),
                  _const_spec(g1.shape), _const_spec(b1.shape),
                  _const_spec(wrh.shape), _const_spec(wrl.shape), _const_spec(br.shape)],
        out_specs=[row(d), pl.BlockSpec((tm * rpt, LANES), lambda i: (i, 0)), row(LANES), row(LANES),
                   _const_spec((SUBLANES, LANES))],
        out_shape=[jax.ShapeDtypeStruct((n, d), F32), jax.ShapeDtypeStruct((n * rpt, LANES), jnp.uint32),
                   jax.ShapeDtypeStruct((n, LANES), jnp.int32), jax.ShapeDtypeStruct((n, LANES), F32),
                   jax.ShapeDtypeStruct((SUBLANES, LANES), F32)],
        scratch_shapes=[pltpu.VMEM((1, LANES), F32)],
        compiler_params=_params(("arbitrary",)),
    )(z, yc, x2d, wglu, bglu, wo, bo, g1, b1, wrh, wrl, br)


MOE_RMW_BATCH = 4


def _moe_kernel(te_ref, vt_ref, na_ref, src_ref, ws_ref, hp_hbm, wg_ref, wu_ref, wd_ref, m_hbm,
                hp_v, acc_v, xs, ys, wgb, wub, wdb, sem, *, tm, row_stride, rpt, n_half, half):
    t = pl.program_id(0)
    na = na_ref[0]
    d_half = rpt * LANES
    tok_rows = n_half * rpt

    @pl.when(t == 0)
    def _():
        load = pltpu.make_async_copy(hp_hbm.at[pl.ds(half * tok_rows, tok_rows), :],
                                     hp_v.at[pl.ds(0, tok_rows), :], sem.at[0])
        load.start()
        acc_v[...] = jnp.zeros_like(acc_v)
        xs[...] = jnp.zeros_like(xs)
        hp_v[tok_rows:tok_rows + rpt, :] = jnp.zeros((rpt, LANES), jnp.uint32)
        load.wait()

    new_expert = (t == 0) | (te_ref[t] != te_ref[jnp.maximum(t - 1, 0)])

    @pl.when(new_expert & (t < na))
    def _():
        wgb[...] = wg_ref[0].astype(BF16)
        wub[...] = wu_ref[0].astype(BF16)
        wdb[...] = wd_ref[0].astype(BF16)

    @pl.when(t < na)
    def _():
        valid = vt_ref[t]

        def gather(i, c):
            for j in range(SUBLANES):
                r = i * SUBLANES + j
                row0 = pl.multiple_of(src_ref[0, 0, r] * rpt, rpt)
                xs[pl.ds(r, rpt, stride=row_stride), :] = hp_v[pl.ds(row0, rpt), :]
            return c
        lax.fori_loop(0, lax.shift_right_logical(valid + (SUBLANES - 1), 3), gather, 0)

        lo, hi = [], []
        for c in range(rpt):
            a, b = _unpack_pairs(xs[c * row_stride:c * row_stride + tm, :])
            lo.append(a.astype(BF16))
            hi.append(b.astype(BF16))
        x_lo = jnp.concatenate(lo, axis=-1)
        x_hi = jnp.concatenate(hi, axis=-1)
        g = _dot(x_lo, wgb[0:d_half, :]) + _dot(x_hi, wgb[d_half:, :])
        u = _dot(x_lo, wub[0:d_half, :]) + _dot(x_hi, wub[d_half:, :])
        act = ((g * jax.nn.sigmoid(g)) * u).astype(BF16)
        y = _dot(act, wdb[...])
        for c in range(2 * rpt):
            ys[c * row_stride:c * row_stride + tm, :] = y[:, c * LANES:(c + 1) * LANES]

        def accumulate(i, c):
            pending = []
            for j in range(MOE_RMW_BATCH):
                r = i * MOE_RMW_BATCH + j
                row0 = pl.multiple_of(src_ref[0, 0, r] * rpt, rpt)
                pending.append((row0, ws_ref[0, 0, r], acc_v[pl.ds(row0, rpt), :],
                                ys[pl.ds(r, rpt, stride=row_stride), :],
                                ys[pl.ds(rpt * row_stride + r, rpt, stride=row_stride), :]))
            for row0, w, words, y_lo, y_hi in pending:
                a_lo, a_hi = _unpack_pairs(words)
                acc_v[pl.ds(row0, rpt), :] = _pack_pairs(a_lo + w * y_lo, a_hi + w * y_hi)
            return c
        lax.fori_loop(0, lax.shift_right_logical(valid + (MOE_RMW_BATCH - 1), 2), accumulate, 0)

    @pl.when(t == pl.num_programs(0) - 1)
    def _():
        store = pltpu.make_async_copy(acc_v.at[pl.ds(0, tok_rows), :], m_hbm, sem.at[1])
        store.start()
        store.wait()


def _moe_half(tile_e, valid, n_active, src, ws, hp, w_gate, w_up, w_down, *, layer, half, n_half, tm):
    _, n_exp, d, f = w_gate.shape
    rpt = d // 2 // LANES
    n_tiles = tile_e.shape[0]
    row_stride = tm + SUBLANES
    smem_row = pl.BlockSpec((1, 1, tm), lambda t, te, vt, na: (t, 0, 0), memory_space=pltpu.SMEM)
    wspec = lambda a, b: pl.BlockSpec((None, 1, a, b), lambda t, te, vt, na: (layer, te[t], 0, 0))
    return pl.pallas_call(
        functools.partial(_moe_kernel, tm=tm, row_stride=row_stride, rpt=rpt, n_half=n_half, half=half),
        grid_spec=pltpu.PrefetchScalarGridSpec(
            num_scalar_prefetch=3,
            grid=(n_tiles,),
            in_specs=[smem_row, smem_row, pl.BlockSpec(memory_space=pl.ANY), wspec(d, f), wspec(d, f), wspec(f, d)],
            out_specs=pl.BlockSpec(memory_space=pl.ANY),
            scratch_shapes=[
                pltpu.VMEM(((n_half + 1) * rpt, LANES), jnp.uint32),
                pltpu.VMEM(((n_half + 1) * rpt, LANES), jnp.uint32),
                pltpu.VMEM((rpt * row_stride, LANES), jnp.uint32),
                pltpu.VMEM((2 * rpt * row_stride, LANES), F32),
                pltpu.VMEM((d, f), BF16), pltpu.VMEM((d, f), BF16), pltpu.VMEM((f, d), BF16),
                pltpu.SemaphoreType.DMA((2,)),
            ],
        ),
        out_shape=jax.ShapeDtypeStruct((n_half * rpt, LANES), jnp.uint32),
        compiler_params=_params(("arbitrary",)),
    )(tile_e, valid, n_active, src, ws, hp, w_gate, w_up, w_down)


def _dispatch_plan(ri, rw, cnt, *, n_groups, n_exp, tm, n_half):
    n = ri.shape[0]
    eid = ri[:, 0:EXPERT_TOP_K]
    rank = ri[:, EXPERT_TOP_K:2 * EXPERT_TOP_K]
    counts = cnt[0:2, n_groups:n_groups + n_exp].astype(jnp.int32)
    tiles_per = (counts + tm - 1) // tm
    tile_end = jnp.cumsum(tiles_per, axis=1)
    tile_start = tile_end - tiles_per
    n_active = tile_end[:, -1]
    n_tiles = (n_half * EXPERT_TOP_K) // tm + n_exp
    tok = jnp.arange(n, dtype=jnp.int32)
    half = (tok >= n_half).astype(jnp.int32)
    slot = (half[:, None] * n_tiles + tile_start[half[:, None], eid]) * tm + rank
    assign = jnp.full((2 * n_tiles * tm,), -1, jnp.int32).at[slot.reshape(-1)].set(
        jnp.arange(n * EXPERT_TOP_K, dtype=jnp.int32))
    filled = assign >= 0
    a = jnp.maximum(assign, 0)
    src = jnp.where(filled, (tok - half * n_half)[a // EXPERT_TOP_K], n_half)
    ws = jnp.where(filled, rw[:, 0:EXPERT_TOP_K].reshape(-1)[a], 0.0)
    t_idx = jnp.arange(n_tiles, dtype=jnp.int32)[None, :]
    t_act = jnp.minimum(t_idx, n_active[:, None] - 1)
    tile_e = jnp.minimum(jnp.sum(t_act[:, :, None] >= tile_end[:, None, :], axis=-1), n_exp - 1).astype(jnp.int32)
    cnt_t = jnp.take_along_axis(counts, tile_e, axis=1)
    start_t = jnp.take_along_axis(tile_start, tile_e, axis=1)
    valid = jnp.where(t_idx < n_active[:, None], jnp.clip(cnt_t - (t_idx - start_t) * tm, 0, tm), 0)
    return (tile_e, valid.astype(jnp.int32), n_active.astype(jnp.int32),
            src.reshape(2, n_tiles, 1, tm).astype(jnp.int32), ws.reshape(2, n_tiles, 1, tm))


def _post_kernel(m0_ref, m1_ref, h_ref, p_ref, wp_ref, wpg_ref, bpg_ref,
                 g2_ref, b2_ref, g3_ref, b3_ref, o_ref, *, alpha):
    tm, d = h_ref.shape
    rpt = d // 2 // LANES
    first_half = pl.program_id(0) < pl.num_programs(0) // 2
    lo, hi = [], []
    for c in range(rpt):
        rows = pl.ds(c, tm, stride=rpt)
        a, b = _unpack_pairs(jnp.where(first_half, m0_ref[rows, :], m1_ref[rows, :]))
        lo.append(a)
        hi.append(b)
    m = jnp.concatenate(lo + hi, axis=-1)
    h2 = _layer_norm(alpha * h_ref[...] + m, g2_ref[...], b2_ref[...])
    gate = jax.nn.sigmoid(_dot(h2.astype(BF16), wpg_ref[...]) + bpg_ref[...])
    e = _dot(p_ref[...].astype(BF16), wp_ref[...]) * gate
    o_ref[...] = _layer_norm(alpha * h2 + e, g3_ref[...], b3_ref[...])


def _post(m0, m1, h, p3d, wp, wpg, bpg, g2, b2, g3, b3, *, layer, alpha, tm):
    n, d = h.shape
    ple = p3d.shape[2]
    rpt = d // 2 // LANES
    steps_half = n // tm // 2
    row = lambda w: pl.BlockSpec((tm, w), lambda i: (i, 0))
    return pl.pallas_call(
        functools.partial(_post_kernel, alpha=alpha),
        grid=(n // tm,),
        in_specs=[pl.BlockSpec((tm * rpt, LANES), lambda i: (jnp.minimum(i, steps_half - 1), 0)),
                  pl.BlockSpec((tm * rpt, LANES), lambda i: (jnp.maximum(i - steps_half, 0), 0)),
                  row(d), pl.BlockSpec((None, tm, ple), lambda i: (layer, i, 0)),
                  _const_spec(wp.shape), _const_spec(wpg.shape), _const_spec(bpg.shape),
                  _const_spec(g2.shape), _const_spec(b2.shape), _const_spec(g3.shape), _const_spec(b3.shape)],
        out_specs=row(d),
        out_shape=jax.ShapeDtypeStruct((n, d), F32),
        compiler_params=_params(("parallel",)),
    )(m0, m1, h, p3d, wp, wpg, bpg, g2, b2, g3, b3)


def _tiles(n, seq):
    return dict(
        inproj_tm=min(512, n),
        s5_tc=min(512, seq),
        conv_tt=min(128, seq),
        mix_tm=min(256, n),
        moe_tm=min(256, n),
        post_tm=min(256, n),
    )


def kernel(x, p, w_in, b_in, lam_re, lam_im, log_dt, ssm_b_re, ssm_b_im, ssm_c_re, ssm_c_im, ssm_d, w_glu, b_glu, w_dw, b_dw, conv_ln_g, conv_ln_b, w_o, b_o, ln1_g, ln1_b, w_rg, b_rg, w_re, b_re, w_gate, w_up, w_down, ln2_g, ln2_b, w_p, w_pg, b_pg, ln3_g, ln3_b):
    depth = w_in.shape[0]
    nb, seq, d = x.shape
    n = nb * seq
    ssm_w = w_glu.shape[1]
    conv_w = w_dw.shape[2]
    n_groups = w_rg.shape[2]
    n_exp = w_re.shape[2]
    per_group = n_exp // n_groups
    alpha = (2 * depth) ** 0.25
    tl = _tiles(n, seq)
    row2 = lambda v: v.reshape(1, -1)

    x2d = x.reshape(n, d)
    p3d = p.reshape(depth, n, p.shape[-1])
    for i in range(depth):
        u, hc = _inproj(x2d, w_in[i].astype(BF16), row2(b_in[i]), ssm_w=ssm_w, conv_w=conv_w, tm=tl['inproj_tm'])

        bm, cm, ar, ai = _s5_params(lam_re[i], lam_im[i], log_dt[i], ssm_b_re[i], ssm_b_im[i],
                                    ssm_c_re[i], ssm_c_im[i], nb)
        z = _s5(u.reshape(nb, seq, ssm_w), bm, cm, ar, ai, row2(ssm_d[i]), tc=tl['s5_tc']).reshape(n, ssm_w)

        yc = _conv(hc.reshape(nb, seq, conv_w), w_dw[i], row2(b_dw[i]), row2(conv_ln_g[i]), row2(conv_ln_b[i]),
                   tt=tl['conv_tt']).reshape(n, conv_w)

        wr = jnp.concatenate([w_rg[i], w_re[i]], axis=1)
        wr = jnp.pad(wr, ((0, 0), (0, LANES - wr.shape[1])))
        wrh = wr.astype(BF16)
        wrl = (wr - wrh.astype(F32)).astype(BF16)
        br = jnp.pad(jnp.concatenate([b_rg[i], b_re[i]]), (0, LANES - n_groups - n_exp)).reshape(1, LANES)
        h, hp, ri, rw, cnt = _mix_out(z, yc, x2d, w_glu[i].astype(BF16), row2(b_glu[i]), w_o[i].astype(BF16),
                                      row2(b_o[i]), row2(ln1_g[i]), row2(ln1_b[i]), wrh, wrl, br,
                                      alpha=alpha, n_groups=n_groups, per_group=per_group, tm=tl['mix_tm'])

        n_half = n // 2
        tile_e, valid, n_active, src, ws = _dispatch_plan(ri, rw, cnt, n_groups=n_groups, n_exp=n_exp,
                                                          tm=tl['moe_tm'], n_half=n_half)
        mix = [_moe_half(tile_e[k], valid[k], n_active[k:k + 1], src[k], ws[k], hp, w_gate, w_up, w_down,
                         layer=i, half=k, n_half=n_half, tm=tl['moe_tm']) for k in range(2)]

        x2d = _post(mix[0], mix[1], h, p3d, w_p[i].astype(BF16), w_pg[i].astype(BF16), row2(b_pg[i]),
                    row2(ln2_g[i]), row2(ln2_b[i]), row2(ln3_g[i]), row2(ln3_b[i]),
                    layer=i, alpha=alpha, tm=tl['post_tm'])
    return x2d.reshape(nb, seq, d)
```

```python
import functools
import math

import jax
import jax.numpy as jnp
from jax import lax
from jax.experimental import pallas as pl
from jax.experimental.pallas import tpu as pltpu

BF16 = jnp.bfloat16
F32 = jnp.float32

CH_PER_GROUP = 16
EXPERT_TOP_K = 2
LN_EPS = 1e-5
LAMBDA_RE_MAX = -1e-4

LANES = 128
SUBLANES = 8
V7X_VMEM_BYTES = 64 * 1024 * 1024
VMEM_LIMIT_BYTES = V7X_VMEM_BYTES - 8 * 1024 * 1024

GROUPS_PER_SET = 8
SETS_PER_STEP = 2


def _params(semantics):
    return pltpu.CompilerParams(dimension_semantics=semantics, vmem_limit_bytes=VMEM_LIMIT_BYTES)


def _const_spec(shape):
    nd = len(shape)
    return pl.BlockSpec(shape, lambda *_: (0,) * nd)


def _layer_norm(v, g, b):
    mu = jnp.mean(v, axis=-1, keepdims=True)
    vc = v - mu
    var = jnp.mean(vc * vc, axis=-1, keepdims=True)
    return vc * lax.rsqrt(var + LN_EPS) * g + b


def _dot(a, b):
    return jnp.dot(a, b, preferred_element_type=F32)


def _inproj_kernel(x_ref, w_ref, b_ref, u_ref, hc_ref, *, ssm_w, conv_w):
    xb = x_ref[...].astype(BF16)

    def proj(lo, width):
        return _dot(xb, w_ref[:, lo:lo + width]) + b_ref[:, lo:lo + width]

    u_ref[...] = proj(0, ssm_w)
    v = proj(ssm_w, conv_w)
    g = proj(ssm_w + conv_w, conv_w)
    hc_ref[...] = v * jax.nn.sigmoid(g)


def _inproj(x2d, w_bf, b, *, ssm_w, conv_w, tm):
    n, d = x2d.shape
    in_w = w_bf.shape[1]
    return pl.pallas_call(
        functools.partial(_inproj_kernel, ssm_w=ssm_w, conv_w=conv_w),
        grid=(n // tm,),
        in_specs=[pl.BlockSpec((tm, d), lambda i: (i, 0)), _const_spec((d, in_w)), _const_spec((1, in_w))],
        out_specs=[pl.BlockSpec((tm, ssm_w), lambda i: (i, 0)), pl.BlockSpec((tm, conv_w), lambda i: (i, 0))],
        out_shape=[jax.ShapeDtypeStruct((n, ssm_w), F32), jax.ShapeDtypeStruct((n, conv_w), F32)],
        compiler_params=_params(("parallel",)),
    )(x2d, w_bf, b)


def _gelu_tanh(y):
    c = math.sqrt(2.0 / math.pi)
    return y * (0.5 * (1.0 + jnp.tanh(c * (y + 0.044715 * (y * y * y)))))


def _s5_kernel(u_ref, bm_ref, cm_ref, ar_ref, ai_ref, d_ref, z_ref, s_ref, xr_ref, xi_ref,
               *, tc, unroll, nblk):
    nb = u_ref.shape[0]
    set_ch = GROUPS_PER_SET * CH_PER_GROUP
    n_lt = s_ref.shape[0] // nblk
    n_c = n_lt // 2
    rows = SETS_PER_STEP * nb
    lt = lambda j: slice(j * LANES, (j + 1) * LANES)
    ch = lambda q, s: slice((q * SETS_PER_STEP + s) * set_ch, (q * SETS_PER_STEP + s + 1) * set_ch)
    slabs = [(q, s, b) for q in range(nblk) for s in range(SETS_PER_STEP) for b in range(nb)]

    @pl.when(pl.program_id(1) == 0)
    def _():
        xr_ref[...] = jnp.zeros_like(xr_ref)
        xi_ref[...] = jnp.zeros_like(xi_ref)

    slab_rows = lambda s, b: pl.ds(s * nb + b, tc, stride=rows)

    for q, s, b in slabs:
        bu = _dot(u_ref[b, :, ch(q, s)].astype(BF16), bm_ref[q * SETS_PER_STEP + s])
        for j in range(n_lt):
            s_ref[q * n_lt + j, slab_rows(s, b), :] = bu[:, lt(j)]

    chains = [(q, j) for q in range(nblk) for j in range(n_c)]

    def step(t, carry):
        rows_t = pl.ds(pl.multiple_of(t * rows, rows), rows)
        out = []
        for (q, j), (xr, xi) in zip(chains, carry):
            ar = ar_ref[q, :, lt(j)]
            ai = ai_ref[q, :, lt(j)]
            nxr = ar * xr - ai * xi + s_ref[q * n_lt + j, rows_t, :]
            nxi = ar * xi + ai * xr + s_ref[q * n_lt + n_c + j, rows_t, :]
            s_ref[q * n_lt + j, rows_t, :] = nxr
            s_ref[q * n_lt + n_c + j, rows_t, :] = nxi
            out.append((nxr, nxi))
        return tuple(out)

    def block(i, carry):
        for j in range(unroll):
            carry = step(i * unroll + j, carry)
        return carry

    init = tuple((xr_ref[q, :, lt(j)], xi_ref[q, :, lt(j)]) for q, j in chains)
    final = lax.fori_loop(0, tc // unroll, block, init)
    for (q, j), (xr, xi) in zip(chains, final):
        xr_ref[q, :, lt(j)] = xr
        xi_ref[q, :, lt(j)] = xi

    for q, s, b in slabs:
        st = jnp.concatenate([s_ref[q * n_lt + j, slab_rows(s, b), :].astype(BF16) for j in range(n_lt)], axis=-1)
        y = _dot(st, cm_ref[q * SETS_PER_STEP + s]) + d_ref[:, ch(q, s)] * u_ref[b, :, ch(q, s)]
        z_ref[b, :, ch(q, s)] = _gelu_tanh(y)


def _s5(u3d, bm, cm, ar, ai, d, *, tc, nblk):
    nb, seq, ssm_w = u3d.shape
    step_ch = nblk * SETS_PER_STEP * GROUPS_PER_SET * CH_PER_GROUP
    state_w = bm.shape[-1]
    n_lt = state_w // LANES
    assert SETS_PER_STEP * nb == SUBLANES, "the scan fills the 8 sublanes with (set, batch)"
    lam_spec = pl.BlockSpec((nblk, SUBLANES, state_w // 2), lambda g, t: (g, 0, 0))
    return pl.pallas_call(
        functools.partial(_s5_kernel, tc=tc, unroll=8, nblk=nblk),
        grid=(ssm_w // step_ch, seq // tc),
        in_specs=[
            pl.BlockSpec((nb, tc, step_ch), lambda g, t: (0, t, g)),
            pl.BlockSpec((nblk * SETS_PER_STEP, bm.shape[1], state_w), lambda g, t: (g, 0, 0)),
            pl.BlockSpec((nblk * SETS_PER_STEP, state_w, cm.shape[2]), lambda g, t: (g, 0, 0)),
            lam_spec, lam_spec,
            pl.BlockSpec((1, step_ch), lambda g, t: (0, g)),
        ],
        out_specs=pl.BlockSpec((nb, tc, step_ch), lambda g, t: (0, t, g)),
        out_shape=jax.ShapeDtypeStruct((nb, seq, ssm_w), F32),
        scratch_shapes=[
            pltpu.VMEM((nblk * n_lt, SUBLANES * tc, LANES), F32),
            pltpu.VMEM((nblk, SUBLANES, state_w // 2), F32),
            pltpu.VMEM((nblk, SUBLANES, state_w // 2), F32),
        ],
        compiler_params=_params(("parallel", "arbitrary")),
    )(u3d, bm, cm, ar, ai, d)


def _s5_params(lam_re, lam_im, log_dt, b_re, b_im, c_re, c_im, nb):
    g, p = lam_re.shape
    lam = lax.complex(jnp.minimum(lam_re, LAMBDA_RE_MAX), lam_im)
    dt = jnp.exp(log_dt)[:, None]
    lam_bar = jnp.exp(lam * dt)
    b_bar = ((lam_bar - 1.0) / lam)[..., None] * lax.complex(b_re, b_im)
    n_set = g // GROUPS_PER_SET
    eye = jnp.eye(GROUPS_PER_SET, dtype=F32)

    def in_mat(part):
        blk = part.reshape(n_set, GROUPS_PER_SET, p, CH_PER_GROUP)
        m = jnp.einsum('sgph,gk->sghkp', blk, eye)
        return m.reshape(n_set, GROUPS_PER_SET * CH_PER_GROUP, GROUPS_PER_SET * p)

    def out_mat(part):
        blk = part.reshape(n_set, GROUPS_PER_SET, CH_PER_GROUP, p)
        m = jnp.einsum('sghp,gk->sgpkh', blk, eye)
        return m.reshape(n_set, GROUPS_PER_SET * p, GROUPS_PER_SET * CH_PER_GROUP)

    bm = jnp.concatenate([in_mat(jnp.real(b_bar)), in_mat(jnp.imag(b_bar))], axis=-1).astype(BF16)
    cm = jnp.concatenate([out_mat(c_re), out_mat(-c_im)], axis=1).astype(BF16)

    def lam_rows(part):
        v = part.reshape(n_set // SETS_PER_STEP, SETS_PER_STEP, 1, GROUPS_PER_SET * p)
        v = jnp.broadcast_to(v, (n_set // SETS_PER_STEP, SETS_PER_STEP, nb, GROUPS_PER_SET * p))
        return v.reshape(n_set // SETS_PER_STEP, SETS_PER_STEP * nb, GROUPS_PER_SET * p)

    return bm, cm, lam_rows(jnp.real(lam_bar)), lam_rows(jnp.imag(lam_bar))


CONV_HALO = 32
CONV_ROWS = 64
CONV_LANES = 256


def _conv_kernel(cur_ref, halo_ref, w_ref, b_ref, g_ref, beta_ref, o_ref, buf_ref, *, tt, taps):
    halo = halo_ref[0]
    width = cur_ref.shape[-1]
    buf_ref[0:CONV_HALO, :] = jnp.where(pl.program_id(1) == 0, jnp.zeros_like(halo), halo)
    buf_ref[CONV_HALO:CONV_HALO + tt, :] = cur_ref[0]
    buf_ref[CONV_HALO + tt:, :] = jnp.zeros((SUBLANES, width), F32)
    lead = CONV_HALO - (taps - 1)
    for c0 in range(0, width, CONV_LANES):
        cs = slice(c0, c0 + CONV_LANES)
        for r0 in range(0, tt, CONV_ROWS):
            acc = jnp.broadcast_to(b_ref[:, cs], (CONV_ROWS, CONV_LANES))
            for s in range(SUBLANES):
                part = None
                for off in range(s, lead + taps, SUBLANES):
                    k = off - lead
                    if k < 0:
                        continue
                    term = buf_ref[r0 + off - s:r0 + off - s + CONV_ROWS + SUBLANES, cs] * w_ref[k:k + 1, cs]
                    part = term if part is None else part + term
                acc = acc + part[s:s + CONV_ROWS]
            o_ref[0, r0:r0 + CONV_ROWS, cs] = acc
    h = _layer_norm(o_ref[0], g_ref[...], beta_ref[...])
    o_ref[0] = h * jax.nn.sigmoid(h)


def _conv(hc3d, w_dw, b_dw, ln_g, ln_b, *, tt):
    nb, seq, width = hc3d.shape
    taps = w_dw.shape[0]
    assert taps - 1 <= CONV_HALO and tt % CONV_HALO == 0
    per = tt // CONV_HALO
    return pl.pallas_call(
        functools.partial(_conv_kernel, tt=tt, taps=taps),
        grid=(nb, seq // tt),
        in_specs=[
            pl.BlockSpec((1, tt, width), lambda b, t: (b, t, 0)),
            pl.BlockSpec((1, CONV_HALO, width), lambda b, t: (b, jnp.maximum(t * per - 1, 0), 0)),
            _const_spec((taps, width)), _const_spec((1, width)), _const_spec((1, width)), _const_spec((1, width)),
        ],
        out_specs=pl.BlockSpec((1, tt, width), lambda b, t: (b, t, 0)),
        out_shape=jax.ShapeDtypeStruct((nb, seq, width), F32),
        scratch_shapes=[pltpu.VMEM((CONV_HALO + tt + SUBLANES, width), F32)],
        compiler_params=_params(("parallel", "parallel")),
    )(hc3d, hc3d, w_dw, b_dw, ln_g, ln_b)


def _first_lane_of_max(vals, lane):
    m = jnp.max(vals, axis=-1, keepdims=True)
    idx = jnp.min(jnp.where(vals == m, lane, LANES), axis=-1, keepdims=True)
    return m, idx


def _pack_pairs(lo, hi):
    lo_bits = lax.bitcast_convert_type(lo.astype(BF16).astype(F32), jnp.uint32)
    hi_bits = lax.bitcast_convert_type(hi.astype(BF16).astype(F32), jnp.uint32)
    return hi_bits | (lo_bits >> 16)


def _unpack_pairs(words):
    lo = lax.bitcast_convert_type(words << 16, F32)
    hi = lax.bitcast_convert_type(words & jnp.uint32(0xFFFF0000), F32)
    return lo, hi


def _mix_out_kernel(z_ref, yc_ref, x_ref, wglu_ref, bglu_ref, wo_ref, bo_ref, g1_ref, b1_ref,
                    wrh_ref, wrl_ref, br_ref, h_ref, hp_ref, ri_ref, rw_ref, cnt_ref, carry_ref,
                    *, alpha, ssm_w, n_groups, per_group):
    tm = z_ref.shape[0]
    d = x_ref.shape[1]
    rpt = d // 2 // LANES
    i = pl.program_id(0)
    steps_half = pl.num_programs(0) // 2

    @pl.when(i == 0)
    def _():
        cnt_ref[...] = jnp.zeros_like(cnt_ref)

    @pl.when((i == 0) | (i == steps_half))
    def _():
        carry_ref[...] = jnp.zeros_like(carry_ref)

    z = z_ref[...]
    zs = z * jax.nn.sigmoid(_dot(z.astype(BF16), wglu_ref[...]) + bglu_ref[...])
    y = (_dot(zs.astype(BF16), wo_ref[0:ssm_w, :]) + _dot(yc_ref[...].astype(BF16), wo_ref[ssm_w:, :])
         + bo_ref[...])
    h = _layer_norm(alpha * x_ref[...] + y, g1_ref[...], b1_ref[...])
    h_ref[...] = h
    for c in range(rpt):
        lo = h[:, c * LANES:(c + 1) * LANES]
        hi = h[:, d // 2 + c * LANES:d // 2 + (c + 1) * LANES]
        hp_ref[pl.ds(c, tm, stride=rpt), :] = _pack_pairs(lo, hi)

    hh = h.astype(BF16)
    hl = (h - hh.astype(F32)).astype(BF16)
    logits = _dot(hh, wrh_ref[...]) + (_dot(hh, wrl_ref[...]) + _dot(hl, wrh_ref[...])) + br_ref[...]

    lane = lax.broadcasted_iota(jnp.int32, (tm, LANES), 1)
    neg = -jnp.inf
    gmask = lane < n_groups
    gmax, gidx = _first_lane_of_max(jnp.where(gmask, logits, neg), lane)
    gsum = jnp.sum(jnp.where(gmask, jnp.exp(logits - gmax), 0.0), axis=-1, keepdims=True)
    g_p = 1.0 / gsum
    elo = n_groups + gidx * per_group
    le = jnp.where((lane >= elo) & (lane < elo + per_group), logits, neg)
    v1, i1 = _first_lane_of_max(le, lane)
    v2, i2 = _first_lane_of_max(jnp.where(lane == i1, neg, le), lane)
    e2 = jnp.exp(v2 - v1)
    w1 = g_p / (1.0 + e2)
    w2 = w1 * e2

    oh1 = lane == i1
    oh2 = lane == i2
    oh = jnp.where(oh1 | oh2, 1.0, 0.0)
    row = lax.broadcasted_iota(jnp.int32, (tm, tm), 0)
    col = lax.broadcasted_iota(jnp.int32, (tm, tm), 1)
    earlier = jnp.where(col < row, 1.0, 0.0).astype(BF16)
    base = _dot(earlier, oh.astype(BF16)) + carry_ref[...]
    r1 = jnp.sum(jnp.where(oh1, base, 0.0), axis=-1, keepdims=True).astype(jnp.int32)
    r2 = jnp.sum(jnp.where(oh2, base, 0.0), axis=-1, keepdims=True).astype(jnp.int32)
    carry_ref[...] += jnp.sum(oh, axis=0, keepdims=True)

    @pl.when(i < steps_half)
    def _():
        cnt_ref[0:1, :] = carry_ref[...]

    @pl.when(i >= steps_half)
    def _():
        cnt_ref[1:2, :] = carry_ref[...]

    ri_ref[...] = jnp.where(lane == 0, i1 - n_groups,
                            jnp.where(lane == 1, i2 - n_groups,
                                      jnp.where(lane == 2, r1, jnp.where(lane == 3, r2, 0))))
    rw_ref[...] = jnp.where(lane == 0, w1, jnp.where(lane == 1, w2, 0.0))


def _mix_out(z, yc, x2d, wglu, bglu, wo, bo, g1, b1, wrh, wrl, br, *, alpha, n_groups, per_group, tm):
    n, d = x2d.shape
    ssm_w = z.shape[1]
    conv_w = yc.shape[1]
    rpt = d // 2 // LANES
    assert (n // tm) % 2 == 0
    row = lambda w: pl.BlockSpec((tm, w), lambda i: (i, 0))
    return pl.pallas_call(
        functools.partial(_mix_out_kernel, alpha=alpha, ssm_w=ssm_w, n_groups=n_groups, per_group=per_group),
        grid=(n // tm,),
        in_specs=[row(ssm_w), row(conv_w), row(d),
                  _const_spec(wglu.shape), _const_spec(bglu.shape), _const_spec(wo.shape), _const_spec(bo.shape),
                  _const_spec(g1.shape), _const_spec(b1.shape),
                  _const_spec(wrh.shape), _const_spec(wrl.shape), _const_spec(br.shape)],
        out_specs=[row(d), pl.BlockSpec((tm * rpt, LANES), lambda i: (i, 0)), row(LANES), row(LANES),
                   _const_spec((SUBLANES, LANES))],
        out_shape=[jax.ShapeDtypeStruct((n, d), F32), jax.ShapeDtypeStruct((n * rpt, LANES), jnp.uint32),
                   jax.ShapeDtypeStruct((n, LANES), jnp.int32), jax.ShapeDtypeStruct((n, LANES), F32),
                   jax.ShapeDtypeStruct((SUBLANES, LANES), F32)],
        scratch_shapes=[pltpu.VMEM((1, LANES), F32)],
        compiler_params=_params(("arbitrary",)),
    )(z, yc, x2d, wglu, bglu, wo, bo, g1, b1, wrh, wrl, br)


MOE_RMW_BATCH = 4


def _moe_kernel(te_ref, vt_ref, na_ref, src_ref, ws_ref, hp_hbm, wg_ref, wu_ref, wd_ref, m_hbm,
                hp_v, acc_v, xs, ys, wgb, wub, wdb, sem, *, tm, row_stride, rpt, n_half, half):
    t = pl.program_id(0)
    na = na_ref[0]
    d_half = rpt * LANES
    tok_rows = n_half * rpt

    @pl.when(t == 0)
    def _():
        load = pltpu.make_async_copy(hp_hbm.at[pl.ds(half * tok_rows, tok_rows), :],
                                     hp_v.at[pl.ds(0, tok_rows), :], sem.at[0])
        load.start()
        acc_v[...] = jnp.zeros_like(acc_v)
        xs[...] = jnp.zeros_like(xs)
        hp_v[tok_rows:tok_rows + rpt, :] = jnp.zeros((rpt, LANES), jnp.uint32)
        load.wait()

    new_expert = (t == 0) | (te_ref[t] != te_ref[jnp.maximum(t - 1, 0)])

    @pl.when(new_expert & (t < na))
    def _():
        wgb[...] = wg_ref[0].astype(BF16)
        wub[...] = wu_ref[0].astype(BF16)
        wdb[...] = wd_ref[0].astype(BF16)

    @pl.when(t < na)
    def _():
        valid = vt_ref[t]

        def gather(i, c):
            for j in range(SUBLANES):
                r = i * SUBLANES + j
                row0 = pl.multiple_of(src_ref[0, 0, r] * rpt, rpt)
                xs[pl.ds(r, rpt, stride=row_stride), :] = hp_v[pl.ds(row0, rpt), :]
            return c
        lax.fori_loop(0, lax.shift_right_logical(valid + (SUBLANES - 1), 3), gather, 0)

        lo, hi = [], []
        for c in range(rpt):
            a, b = _unpack_pairs(xs[c * row_stride:c * row_stride + tm, :])
            lo.append(a.astype(BF16))
            hi.append(b.astype(BF16))
        x_lo = jnp.concatenate(lo, axis=-1)
        x_hi = jnp.concatenate(hi, axis=-1)
        g = _dot(x_lo, wgb[0:d_half, :]) + _dot(x_hi, wgb[d_half:, :])
        u = _dot(x_lo, wub[0:d_half, :]) + _dot(x_hi, wub[d_half:, :])
        act = ((g * jax.nn.sigmoid(g)) * u).astype(BF16)
        y = _dot(act, wdb[...])
        for c in range(2 * rpt):
            ys[c * row_stride:c * row_stride + tm, :] = y[:, c * LANES:(c + 1) * LANES]

        def accumulate(i, c):
            pending = []
            for j in range(MOE_RMW_BATCH):
                r = i * MOE_RMW_BATCH + j
                row0 = pl.multiple_of(src_ref[0, 0, r] * rpt, rpt)
                pending.append((row0, ws_ref[0, 0, r], acc_v[pl.ds(row0, rpt), :],
                                ys[pl.ds(r, rpt, stride=row_stride), :],
                                ys[pl.ds(rpt * row_stride + r, rpt, stride=row_stride), :]))
            for row0, w, words, y_lo, y_hi in pending:
                a_lo, a_hi = _unpack_pairs(words)
                acc_v[pl.ds(row0, rpt), :] = _pack_pairs(a_lo + w * y_lo, a_hi + w * y_hi)
            return c
        lax.fori_loop(0, lax.shift_right_logical(valid + (MOE_RMW_BATCH - 1), 2), accumulate, 0)

    @pl.when(t == pl.num_programs(0) - 1)
    def _():
        store = pltpu.make_async_copy(acc_v.at[pl.ds(0, tok_rows), :], m_hbm, sem.at[1])
        store.start()
        store.wait()


def _moe_half(tile_e, valid, n_active, src, ws, hp, w_gate, w_up, w_down, *, layer, half, n_half, tm):
    _, n_exp, d, f = w_gate.shape
    rpt = d // 2 // LANES
    n_tiles = tile_e.shape[0]
    row_stride = tm + SUBLANES
    smem_row = pl.BlockSpec((1, 1, tm), lambda t, te, vt, na: (t, 0, 0), memory_space=pltpu.SMEM)
    wspec = lambda a, b: pl.BlockSpec((None, 1, a, b), lambda t, te, vt, na: (layer, te[t], 0, 0))
    return pl.pallas_call(
        functools.partial(_moe_kernel, tm=tm, row_stride=row_stride, rpt=rpt, n_half=n_half, half=half),
        grid_spec=pltpu.PrefetchScalarGridSpec(
            num_scalar_prefetch=3,
            grid=(n_tiles,),
            in_specs=[smem_row, smem_row, pl.BlockSpec(memory_space=pl.ANY), wspec(d, f), wspec(d, f), wspec(f, d)],
            out_specs=pl.BlockSpec(memory_space=pl.ANY),
            scratch_shapes=[
                pltpu.VMEM(((n_half + 1) * rpt, LANES), jnp.uint32),
                pltpu.VMEM(((n_half + 1) * rpt, LANES), jnp.uint32),
                pltpu.VMEM((rpt * row_stride, LANES), jnp.uint32),
                pltpu.VMEM((2 * rpt * row_stride, LANES), F32),
                pltpu.VMEM((d, f), BF16), pltpu.VMEM((d, f), BF16), pltpu.VMEM((f, d), BF16),
                pltpu.SemaphoreType.DMA((2,)),
            ],
        ),
        out_shape=jax.ShapeDtypeStruct((n_half * rpt, LANES), jnp.uint32),
        compiler_params=_params(("arbitrary",)),
    )(tile_e, valid, n_active, src, ws, hp, w_gate, w_up, w_down)


def _dispatch_plan(ri, rw, cnt, *, n_groups, n_exp, tm, n_half):
    n = ri.shape[0]
    eid = ri[:, 0:EXPERT_TOP_K]
    rank = ri[:, EXPERT_TOP_K:2 * EXPERT_TOP_K]
    counts = cnt[0:2, n_groups:n_groups + n_exp].astype(jnp.int32)
    tiles_per = (counts + tm - 1) // tm
    tile_end = jnp.cumsum(tiles_per, axis=1)
    tile_start = tile_end - tiles_per
    n_active = tile_end[:, -1]
    n_tiles = (n_half * EXPERT_TOP_K) // tm + n_exp
    tok = jnp.arange(n, dtype=jnp.int32)
    half = (tok >= n_half).astype(jnp.int32)
    experts = jnp.arange(n_exp, dtype=jnp.int32)

    def lookup(table, idx):
        return jnp.sum(jnp.where(idx[..., None] == experts, table[..., None, :], 0), axis=-1)

    start_tok = lookup(jnp.where(half[:, None] == 0, tile_start[0][None, :], tile_start[1][None, :]), eid)
    slot = (half[:, None] * n_tiles + start_tok) * tm + rank
    payload = jnp.stack([jnp.broadcast_to((tok - half * n_half)[:, None], slot.shape),
                         lax.bitcast_convert_type(rw[:, 0:EXPERT_TOP_K], jnp.int32)], axis=-1)
    empty = jnp.broadcast_to(jnp.array([n_half, 0], jnp.int32), (2 * n_tiles * tm, 2))
    table = empty.at[slot.reshape(-1)].set(payload.reshape(-1, 2))
    src = table[:, 0]
    ws = lax.bitcast_convert_type(table[:, 1], F32)
    t_idx = jnp.arange(n_tiles, dtype=jnp.int32)[None, :]
    t_act = jnp.minimum(t_idx, n_active[:, None] - 1)
    tile_e = jnp.minimum(jnp.sum(t_act[:, :, None] >= tile_end[:, None, :], axis=-1), n_exp - 1).astype(jnp.int32)
    cnt_t = lookup(counts, tile_e)
    start_t = lookup(tile_start, tile_e)
    valid = jnp.where(t_idx < n_active[:, None], jnp.clip(cnt_t - (t_idx - start_t) * tm, 0, tm), 0)
    return (tile_e, valid.astype(jnp.int32), n_active.astype(jnp.int32),
            src.reshape(2, n_tiles, 1, tm), ws.reshape(2, n_tiles, 1, tm))


def _post_kernel(m0_ref, m1_ref, h_ref, p_ref, wp_ref, wpg_ref, bpg_ref,
                 g2_ref, b2_ref, g3_ref, b3_ref, o_ref, *, alpha):
    tm, d = h_ref.shape
    rpt = d // 2 // LANES
    first_half = pl.program_id(0) < pl.num_programs(0) // 2
    lo, hi = [], []
    for c in range(rpt):
        rows = pl.ds(c, tm, stride=rpt)
        a, b = _unpack_pairs(jnp.where(first_half, m0_ref[rows, :], m1_ref[rows, :]))
        lo.append(a)
        hi.append(b)
    m = jnp.concatenate(lo + hi, axis=-1)
    h2 = _layer_norm(alpha * h_ref[...] + m, g2_ref[...], b2_ref[...])
    gate = jax.nn.sigmoid(_dot(h2.astype(BF16), wpg_ref[...]) + bpg_ref[...])
    e = _dot(p_ref[...].astype(BF16), wp_ref[...]) * gate
    o_ref[...] = _layer_norm(alpha * h2 + e, g3_ref[...], b3_ref[...])


def _post(m0, m1, h, p3d, wp, wpg, bpg, g2, b2, g3, b3, *, layer, alpha, tm):
    n, d = h.shape
    ple = p3d.shape[2]
    rpt = d // 2 // LANES
    steps_half = n // tm // 2
    row = lambda w: pl.BlockSpec((tm, w), lambda i: (i, 0))
    return pl.pallas_call(
        functools.partial(_post_kernel, alpha=alpha),
        grid=(n // tm,),
        in_specs=[pl.BlockSpec((tm * rpt, LANES), lambda i: (jnp.minimum(i, steps_half - 1), 0)),
                  pl.BlockSpec((tm * rpt, LANES), lambda i: (jnp.maximum(i - steps_half, 0), 0)),
                  row(d), pl.BlockSpec((None, tm, ple), lambda i: (layer, i, 0)),
                  _const_spec(wp.shape), _const_spec(wpg.shape), _const_spec(bpg.shape),
                  _const_spec(g2.shape), _const_spec(b2.shape), _const_spec(g3.shape), _const_spec(b3.shape)],
        out_specs=row(d),
        out_shape=jax.ShapeDtypeStruct((n, d), F32),
        compiler_params=_params(("parallel",)),
    )(m0, m1, h, p3d, wp, wpg, bpg, g2, b2, g3, b3)


def _tiles(n, seq, ssm_w):
    block_ch = SETS_PER_STEP * GROUPS_PER_SET * CH_PER_GROUP
    return dict(
        inproj_tm=min(512, n),
        s5_tc=min(512, seq),
        s5_nblk=2 if ssm_w % (2 * block_ch) == 0 else 1,
        conv_tt=min(128, seq),
        mix_tm=min(256, n),
        moe_tm=min(256, n),
        post_tm=min(256, n),
    )


def kernel(x, p, w_in, b_in, lam_re, lam_im, log_dt, ssm_b_re, ssm_b_im, ssm_c_re, ssm_c_im, ssm_d, w_glu, b_glu, w_dw, b_dw, conv_ln_g, conv_ln_b, w_o, b_o, ln1_g, ln1_b, w_rg, b_rg, w_re, b_re, w_gate, w_up, w_down, ln2_g, ln2_b, w_p, w_pg, b_pg, ln3_g, ln3_b):
    depth = w_in.shape[0]
    nb, seq, d = x.shape
    n = nb * seq
    ssm_w = w_glu.shape[1]
    conv_w = w_dw.shape[2]
    n_groups = w_rg.shape[2]
    n_exp = w_re.shape[2]
    per_group = n_exp // n_groups
    alpha = (2 * depth) ** 0.25
    tl = _tiles(n, seq, ssm_w)
    row2 = lambda v: v.reshape(1, -1)

    x2d = x.reshape(n, d)
    p3d = p.reshape(depth, n, p.shape[-1])
    for i in range(depth):
        u, hc = _inproj(x2d, w_in[i].astype(BF16), row2(b_in[i]), ssm_w=ssm_w, conv_w=conv_w, tm=tl['inproj_tm'])

        bm, cm, ar, ai = _s5_params(lam_re[i], lam_im[i], log_dt[i], ssm_b_re[i], ssm_b_im[i],
                                    ssm_c_re[i], ssm_c_im[i], nb)
        z = _s5(u.reshape(nb, seq, ssm_w), bm, cm, ar, ai, row2(ssm_d[i]), tc=tl['s5_tc'],
                nblk=tl['s5_nblk']).reshape(n, ssm_w)

        yc = _conv(hc.reshape(nb, seq, conv_w), w_dw[i], row2(b_dw[i]), row2(conv_ln_g[i]), row2(conv_ln_b[i]),
                   tt=tl['conv_tt']).reshape(n, conv_w)

        wr = jnp.concatenate([w_rg[i], w_re[i]], axis=1)
        wr = jnp.pad(wr, ((0, 0), (0, LANES - wr.shape[1])))
        wrh = wr.astype(BF16)
        wrl = (wr - wrh.astype(F32)).astype(BF16)
        br = jnp.pad(jnp.concatenate([b_rg[i], b_re[i]]), (0, LANES - n_groups - n_exp)).reshape(1, LANES)
        h, hp, ri, rw, cnt = _mix_out(z, yc, x2d, w_glu[i].astype(BF16), row2(b_glu[i]), w_o[i].astype(BF16),
                                      row2(b_o[i]), row2(ln1_g[i]), row2(ln1_b[i]), wrh, wrl, br,
                                      alpha=alpha, n_groups=n_groups, per_group=per_group, tm=tl['mix_tm'])

        n_half = n // 2
        tile_e, valid, n_active, src, ws = _dispatch_plan(ri, rw, cnt, n_groups=n_groups, n_exp=n_exp,
                                                          tm=tl['moe_tm'], n_half=n_half)
        mix = [_moe_half(tile_e[k], valid[k], n_active[k:k + 1], src[k], ws[k], hp, w_gate, w_up, w_down,
                         layer=i, half=k, n_half=n_half, tm=tl['moe_tm']) for k in range(2)]

        x2d = _post(mix[0], mix[1], h, p3d, w_p[i].astype(BF16), w_pg[i].astype(BF16), row2(b_pg[i]),
                    row2(ln2_g[i]), row2(ln2_b[i]), row2(ln3_g[i]), row2(ln3_b[i]),
                    layer=i, alpha=alpha, tm=tl['post_tm'])
    return x2d.reshape(nb, seq, d)
```

```python
import functools
import math

import jax
import jax.numpy as jnp
from jax import lax
from jax.experimental import pallas as pl
from jax.experimental.pallas import tpu as pltpu

BF16 = jnp.bfloat16
F32 = jnp.float32

CH_PER_GROUP = 16
EXPERT_TOP_K = 2
LN_EPS = 1e-5
LAMBDA_RE_MAX = -1e-4

LANES = 128
SUBLANES = 8
V7X_VMEM_BYTES = 64 * 1024 * 1024
VMEM_LIMIT_BYTES = V7X_VMEM_BYTES - 8 * 1024 * 1024

GROUPS_PER_SET = 8
SETS_PER_STEP = 2


def _params(semantics):
    return pltpu.CompilerParams(dimension_semantics=semantics, vmem_limit_bytes=VMEM_LIMIT_BYTES)


def _const_spec(shape):
    nd = len(shape)
    return pl.BlockSpec(shape, lambda *_: (0,) * nd, pipeline_mode=pl.Buffered(1))


def _layer_norm(v, g, b):
    mu = jnp.mean(v, axis=-1, keepdims=True)
    vc = v - mu
    var = jnp.mean(vc * vc, axis=-1, keepdims=True)
    return vc * lax.rsqrt(var + LN_EPS) * g + b


def _dot(a, b):
    return jnp.dot(a, b, preferred_element_type=F32)


def _inproj_kernel(x_ref, w_ref, b_ref, u_ref, hc_ref, *, ssm_w, conv_w):
    xb = x_ref[...].astype(BF16)

    def proj(lo, width):
        return _dot(xb, w_ref[:, lo:lo + width]) + b_ref[:, lo:lo + width]

    u_ref[...] = proj(0, ssm_w)
    v = proj(ssm_w, conv_w)
    g = proj(ssm_w + conv_w, conv_w)
    hc_ref[...] = v * jax.nn.sigmoid(g)


def _inproj(x2d, w_bf, b, *, ssm_w, conv_w, tm):
    n, d = x2d.shape
    in_w = w_bf.shape[1]
    return pl.pallas_call(
        functools.partial(_inproj_kernel, ssm_w=ssm_w, conv_w=conv_w),
        grid=(n // tm,),
        in_specs=[pl.BlockSpec((tm, d), lambda i: (i, 0)), _const_spec((d, in_w)), _const_spec((1, in_w))],
        out_specs=[pl.BlockSpec((tm, ssm_w), lambda i: (i, 0)), pl.BlockSpec((tm, conv_w), lambda i: (i, 0))],
        out_shape=[jax.ShapeDtypeStruct((n, ssm_w), F32), jax.ShapeDtypeStruct((n, conv_w), F32)],
        compiler_params=_params(("parallel",)),
    )(x2d, w_bf, b)


def _gelu_tanh(y):
    c = math.sqrt(2.0 / math.pi)
    return y * (0.5 * (1.0 + jnp.tanh(c * (y + 0.044715 * (y * y * y)))))


def _s5_kernel(u_ref, bm_ref, cm_ref, ar_ref, ai_ref, d_ref, z_ref, lhs_ref, s_ref, y_ref, xr_ref, xi_ref,
               *, tc, unroll, nblk, mm_rows):
    nb = u_ref.shape[0]
    set_ch = GROUPS_PER_SET * CH_PER_GROUP
    assert set_ch == LANES
    n_lt = s_ref.shape[0] // nblk
    n_c = n_lt // 2
    rows = SETS_PER_STEP * nb
    lt = lambda j: slice(j * LANES, (j + 1) * LANES)
    ch = lambda q, s: slice((q * SETS_PER_STEP + s) * set_ch, (q * SETS_PER_STEP + s + 1) * set_ch)
    slabs = [(s, b) for s in range(SETS_PER_STEP) for b in range(nb)]
    chunks = range(0, tc * rows, mm_rows)

    @pl.when(pl.program_id(1) == 0)
    def _():
        xr_ref[...] = jnp.zeros_like(xr_ref)
        xi_ref[...] = jnp.zeros_like(xi_ref)
        lhs_ref[...] = jnp.zeros_like(lhs_ref)

    slab_rows = lambda s, b: pl.ds(s * nb + b, tc, stride=rows)

    for q in range(nblk):
        for s, b in slabs:
            lhs_ref[q * SETS_PER_STEP + s, slab_rows(s, b), :] = u_ref[b, :, ch(q, s)]
        for r0 in chunks:
            lhs = jnp.concatenate([lhs_ref[q * SETS_PER_STEP + s, r0:r0 + mm_rows, :]
                                   for s in range(SETS_PER_STEP)], axis=-1).astype(BF16)
            bu = _dot(lhs, bm_ref[q])
            for j in range(n_lt):
                s_ref[q * n_lt + j, r0:r0 + mm_rows, :] = bu[:, lt(j)]

    chains = [(q, j) for q in range(nblk) for j in range(n_c)]

    def step(t, carry):
        rows_t = pl.ds(pl.multiple_of(t * rows, rows), rows)
        out = []
        for (q, j), (xr, xi) in zip(chains, carry):
            ar = ar_ref[q, :, lt(j)]
            ai = ai_ref[q, :, lt(j)]
            nxr = ar * xr - ai * xi + s_ref[q * n_lt + j, rows_t, :]
            nxi = ar * xi + ai * xr + s_ref[q * n_lt + n_c + j, rows_t, :]
            s_ref[q * n_lt + j, rows_t, :] = nxr
            s_ref[q * n_lt + n_c + j, rows_t, :] = nxi
            out.append((nxr, nxi))
        return tuple(out)

    def block(i, carry):
        for j in range(unroll):
            carry = step(i * unroll + j, carry)
        return carry

    init = tuple((xr_ref[q, :, lt(j)], xi_ref[q, :, lt(j)]) for q, j in chains)
    final = lax.fori_loop(0, tc // unroll, block, init)
    for (q, j), (xr, xi) in zip(chains, final):
        xr_ref[q, :, lt(j)] = xr
        xi_ref[q, :, lt(j)] = xi

    for q in range(nblk):
        for r0 in chunks:
            st = jnp.concatenate([s_ref[q * n_lt + j, r0:r0 + mm_rows, :].astype(BF16) for j in range(n_lt)],
                                 axis=-1)
            yy = _dot(st, cm_ref[q])
            for s in range(SETS_PER_STEP):
                y_ref[s, r0:r0 + mm_rows, :] = yy[:, lt(s)]
        for s, b in slabs:
            y = y_ref[s, slab_rows(s, b), :] + d_ref[:, ch(q, s)] * u_ref[b, :, ch(q, s)]
            z_ref[b, :, ch(q, s)] = _gelu_tanh(y)


def _s5(u3d, bm, cm, ar, ai, d, *, tc, nblk):
    nb, seq, ssm_w = u3d.shape
    step_ch = nblk * SETS_PER_STEP * GROUPS_PER_SET * CH_PER_GROUP
    state_w = bm.shape[-1]
    n_lt = state_w // LANES
    assert SETS_PER_STEP * nb == SUBLANES, "the scan fills the 8 sublanes with (set, batch)"
    tr = SUBLANES * tc
    lam_spec = pl.BlockSpec((nblk, SUBLANES, state_w // 2), lambda g, t: (g, 0, 0))
    return pl.pallas_call(
        functools.partial(_s5_kernel, tc=tc, unroll=8, nblk=nblk, mm_rows=min(512, tr)),
        grid=(ssm_w // step_ch, seq // tc),
        in_specs=[
            pl.BlockSpec((nb, tc, step_ch), lambda g, t: (0, t, g)),
            pl.BlockSpec((nblk, bm.shape[1], state_w), lambda g, t: (g, 0, 0)),
            pl.BlockSpec((nblk, state_w, cm.shape[2]), lambda g, t: (g, 0, 0)),
            lam_spec, lam_spec,
            pl.BlockSpec((1, step_ch), lambda g, t: (0, g)),
        ],
        out_specs=pl.BlockSpec((nb, tc, step_ch), lambda g, t: (0, t, g)),
        out_shape=jax.ShapeDtypeStruct((nb, seq, ssm_w), F32),
        scratch_shapes=[
            pltpu.VMEM((nblk * SETS_PER_STEP, tr, LANES), F32),
            pltpu.VMEM((nblk * n_lt, tr, LANES), F32),
            pltpu.VMEM((SETS_PER_STEP, tr, LANES), F32),
            pltpu.VMEM((nblk, SUBLANES, state_w // 2), F32),
            pltpu.VMEM((nblk, SUBLANES, state_w // 2), F32),
        ],
        compiler_params=_params(("parallel", "arbitrary")),
    )(u3d, bm, cm, ar, ai, d)


def _s5_params(lam_re, lam_im, log_dt, b_re, b_im, c_re, c_im, nb):
    g, p = lam_re.shape
    lam = lax.complex(jnp.minimum(lam_re, LAMBDA_RE_MAX), lam_im)
    dt = jnp.exp(log_dt)[:, None]
    lam_bar = jnp.exp(lam * dt)
    b_bar = ((lam_bar - 1.0) / lam)[..., None] * lax.complex(b_re, b_im)
    n_set = g // GROUPS_PER_SET
    eye = jnp.eye(GROUPS_PER_SET, dtype=F32)

    def in_mat(part):
        blk = part.reshape(n_set, GROUPS_PER_SET, p, CH_PER_GROUP)
        m = jnp.einsum('sgph,gk->sghkp', blk, eye)
        return m.reshape(n_set, GROUPS_PER_SET * CH_PER_GROUP, GROUPS_PER_SET * p)

    def out_mat(part):
        blk = part.reshape(n_set, GROUPS_PER_SET, CH_PER_GROUP, p)
        m = jnp.einsum('sghp,gk->sgpkh', blk, eye)
        return m.reshape(n_set, GROUPS_PER_SET * p, GROUPS_PER_SET * CH_PER_GROUP)

    bm = jnp.concatenate([in_mat(jnp.real(b_bar)), in_mat(jnp.imag(b_bar))], axis=-1).astype(BF16)
    cm = jnp.concatenate([out_mat(c_re), out_mat(-c_im)], axis=1).astype(BF16)
    n_blk = n_set // SETS_PER_STEP
    bm = bm.reshape(n_blk, SETS_PER_STEP * bm.shape[1], bm.shape[2])
    cm = cm.reshape(n_blk, SETS_PER_STEP, cm.shape[1], cm.shape[2]).transpose(0, 2, 1, 3)
    cm = cm.reshape(n_blk, cm.shape[1], -1)

    def lam_rows(part):
        v = part.reshape(n_set // SETS_PER_STEP, SETS_PER_STEP, 1, GROUPS_PER_SET * p)
        v = jnp.broadcast_to(v, (n_set // SETS_PER_STEP, SETS_PER_STEP, nb, GROUPS_PER_SET * p))
        return v.reshape(n_set // SETS_PER_STEP, SETS_PER_STEP * nb, GROUPS_PER_SET * p)

    return bm, cm, lam_rows(jnp.real(lam_bar)), lam_rows(jnp.imag(lam_bar))


CONV_HALO = 32
CONV_ROWS = 64
CONV_LANES = 256


def _conv_kernel(cur_ref, halo_ref, w_ref, b_ref, g_ref, beta_ref, o_ref, buf_ref, *, tt, taps):
    halo = halo_ref[0]
    width = cur_ref.shape[-1]
    buf_ref[0:CONV_HALO, :] = jnp.where(pl.program_id(1) == 0, jnp.zeros_like(halo), halo)
    buf_ref[CONV_HALO:CONV_HALO + tt, :] = cur_ref[0]
    buf_ref[CONV_HALO + tt:, :] = jnp.zeros((SUBLANES, width), F32)
    lead = CONV_HALO - (taps - 1)
    for c0 in range(0, width, CONV_LANES):
        cs = slice(c0, c0 + CONV_LANES)
        for r0 in range(0, tt, CONV_ROWS):
            acc = jnp.broadcast_to(b_ref[:, cs], (CONV_ROWS, CONV_LANES))
            for s in range(SUBLANES):
                part = None
                for off in range(s, lead + taps, SUBLANES):
                    k = off - lead
                    if k < 0:
                        continue
                    term = buf_ref[r0 + off - s:r0 + off - s + CONV_ROWS + SUBLANES, cs] * w_ref[k:k + 1, cs]
                    part = term if part is None else part + term
                acc = acc + part[s:s + CONV_ROWS]
            o_ref[0, r0:r0 + CONV_ROWS, cs] = acc
    h = _layer_norm(o_ref[0], g_ref[...], beta_ref[...])
    o_ref[0] = h * jax.nn.sigmoid(h)


def _conv(hc3d, w_dw, b_dw, ln_g, ln_b, *, tt):
    nb, seq, width = hc3d.shape
    taps = w_dw.shape[0]
    assert taps - 1 <= CONV_HALO and tt % CONV_HALO == 0
    per = tt // CONV_HALO
    return pl.pallas_call(
        functools.partial(_conv_kernel, tt=tt, taps=taps),
        grid=(nb, seq // tt),
        in_specs=[
            pl.BlockSpec((1, tt, width), lambda b, t: (b, t, 0)),
            pl.BlockSpec((1, CONV_HALO, width), lambda b, t: (b, jnp.maximum(t * per - 1, 0), 0)),
            _const_spec((taps, width)), _const_spec((1, width)), _const_spec((1, width)), _const_spec((1, width)),
        ],
        out_specs=pl.BlockSpec((1, tt, width), lambda b, t: (b, t, 0)),
        out_shape=jax.ShapeDtypeStruct((nb, seq, width), F32),
        scratch_shapes=[pltpu.VMEM((CONV_HALO + tt + SUBLANES, width), F32)],
        compiler_params=_params(("parallel", "parallel")),
    )(hc3d, hc3d, w_dw, b_dw, ln_g, ln_b)


def _first_lane_of_max(vals, lane):
    m = jnp.max(vals, axis=-1, keepdims=True)
    idx = jnp.min(jnp.where(vals == m, lane, LANES), axis=-1, keepdims=True)
    return m, idx


def _pack_pairs(lo, hi):
    lo_bits = lax.bitcast_convert_type(lo.astype(BF16).astype(F32), jnp.uint32)
    hi_bits = lax.bitcast_convert_type(hi.astype(BF16).astype(F32), jnp.uint32)
    return hi_bits | (lo_bits >> 16)


def _unpack_pairs(words):
    lo = lax.bitcast_convert_type(words << 16, F32)
    hi = lax.bitcast_convert_type(words & jnp.uint32(0xFFFF0000), F32)
    return lo, hi


def _mix_out_kernel(z_ref, yc_ref, x_ref, wglu_ref, bglu_ref, wo_ref, bo_ref, g1_ref, b1_ref,
                    wrh_ref, wrl_ref, br_ref, h_ref, hp_ref, ri_ref, rw_ref, cnt_ref, carry_ref,
                    *, alpha, ssm_w, n_groups, per_group):
    tm = z_ref.shape[0]
    d = x_ref.shape[1]
    rpt = d // 2 // LANES
    i = pl.program_id(0)
    steps_half = pl.num_programs(0) // 2

    @pl.when(i == 0)
    def _():
        cnt_ref[...] = jnp.zeros_like(cnt_ref)

    @pl.when((i == 0) | (i == steps_half))
    def _():
        carry_ref[...] = jnp.zeros_like(carry_ref)

    z = z_ref[...]
    zs = z * jax.nn.sigmoid(_dot(z.astype(BF16), wglu_ref[...]) + bglu_ref[...])
    y = (_dot(zs.astype(BF16), wo_ref[0:ssm_w, :]) + _dot(yc_ref[...].astype(BF16), wo_ref[ssm_w:, :])
         + bo_ref[...])
    h = _layer_norm(alpha * x_ref[...] + y, g1_ref[...], b1_ref[...])
    h_ref[...] = h
    for c in range(rpt):
        lo = h[:, c * LANES:(c + 1) * LANES]
        hi = h[:, d // 2 + c * LANES:d // 2 + (c + 1) * LANES]
        hp_ref[pl.ds(c, tm, stride=rpt), :] = _pack_pairs(lo, hi)

    hh = h.astype(BF16)
    hl = (h - hh.astype(F32)).astype(BF16)
    logits = _dot(hh, wrh_ref[...]) + (_dot(hh, wrl_ref[...]) + _dot(hl, wrh_ref[...])) + br_ref[...]

    lane = lax.broadcasted_iota(jnp.int32, (tm, LANES), 1)
    neg = -jnp.inf
    gmask = lane < n_groups
    gmax, gidx = _first_lane_of_max(jnp.where(gmask, logits, neg), lane)
    gsum = jnp.sum(jnp.where(gmask, jnp.exp(logits - gmax), 0.0), axis=-1, keepdims=True)
    g_p = 1.0 / gsum
    elo = n_groups + gidx * per_group
    le = jnp.where((lane >= elo) & (lane < elo + per_group), logits, neg)
    v1, i1 = _first_lane_of_max(le, lane)
    v2, i2 = _first_lane_of_max(jnp.where(lane == i1, neg, le), lane)
    e2 = jnp.exp(v2 - v1)
    w1 = g_p / (1.0 + e2)
    w2 = w1 * e2

    oh1 = lane == i1
    oh2 = lane == i2
    oh = jnp.where(oh1 | oh2, 1.0, 0.0)
    row = lax.broadcasted_iota(jnp.int32, (tm, tm), 0)
    col = lax.broadcasted_iota(jnp.int32, (tm, tm), 1)
    earlier = jnp.where(col < row, 1.0, 0.0).astype(BF16)
    base = _dot(earlier, oh.astype(BF16)) + carry_ref[...]
    r1 = jnp.sum(jnp.where(oh1, base, 0.0), axis=-1, keepdims=True).astype(jnp.int32)
    r2 = jnp.sum(jnp.where(oh2, base, 0.0), axis=-1, keepdims=True).astype(jnp.int32)
    carry_ref[...] += jnp.sum(oh, axis=0, keepdims=True)

    @pl.when(i < steps_half)
    def _():
        cnt_ref[0:1, :] = carry_ref[...]

    @pl.when(i >= steps_half)
    def _():
        cnt_ref[1:2, :] = carry_ref[...]

    ri_ref[...] = jnp.where(lane == 0, i1 - n_groups,
                            jnp.where(lane == 1, i2 - n_groups,
                                      jnp.where(lane == 2, r1, jnp.where(lane == 3, r2, 0))))
    rw_ref[...] = jnp.where(lane == 0, w1, jnp.where(lane == 1, w2, 0.0))


def _mix_out(z, yc, x2d, wglu, bglu, wo, bo, g1, b1, wrh, wrl, br, *, alpha, n_groups, per_group, tm):
    n, d = x2d.shape
    ssm_w = z.shape[1]
    conv_w = yc.shape[1]
    rpt = d // 2 // LANES
    assert (n // tm) % 2 == 0
    row = lambda w: pl.BlockSpec((tm, w), lambda i: (i, 0))
    return pl.pallas_call(
        functools.partial(_mix_out_kernel, alpha=alpha, ssm_w=ssm_w, n_groups=n_groups, per_group=per_group),
        grid=(n // tm,),
        in_specs=[row(ssm_w), row(conv_w), row(d),
                  _const_spec(wglu.shape), _const_spec(bglu.shape), _const_spec(wo.shape), _const_spec(bo.shape),
                  _const_spec(g1.shape), _const_spec(b1.shape),
                  _const_spec(wrh.shape), _const_spec(wrl.shape), _const_spec(br.shape)],
        out_specs=[row(d), pl.BlockSpec((tm * rpt, LANES), lambda i: (i, 0)), row(LANES), row(LANES),
                   _const_spec((SUBLANES, LANES))],
        out_shape=[jax.ShapeDtypeStruct((n, d), F32), jax.ShapeDtypeStruct((n * rpt, LANES), jnp.uint32),
                   jax.ShapeDtypeStruct((n, LANES), jnp.int32), jax.ShapeDtypeStruct((n, LANES), F32),
                   jax.ShapeDtypeStruct((SUBLANES, LANES), F32)],
        scratch_shapes=[pltpu.VMEM((1, LANES), F32)],
        compiler_params=_params(("arbitrary",)),
    )(z, yc, x2d, wglu, bglu, wo, bo, g1, b1, wrh, wrl, br)


MOE_RMW_BATCH = 4


def _moe_kernel(te_ref, vt_ref, na_ref, src_ref, ws_ref, hp_hbm, wg_ref, wu_ref, wd_ref, m_hbm,
                hp_v, acc_v, xs, ys, wgb, wub, wdb, sem, *, tm, row_stride, rpt, n_half, half):
    t = pl.program_id(0)
    na = na_ref[0]
    d_half = rpt * LANES
    tok_rows = n_half * rpt

    @pl.when(t == 0)
    def _():
        load = pltpu.make_async_copy(hp_hbm.at[pl.ds(half * tok_rows, tok_rows), :],
                                     hp_v.at[pl.ds(0, tok_rows), :], sem.at[0])
        load.start()
        acc_v[...] = jnp.zeros_like(acc_v)
        xs[...] = jnp.zeros_like(xs)
        hp_v[tok_rows:tok_rows + rpt, :] = jnp.zeros((rpt, LANES), jnp.uint32)
        load.wait()

    new_expert = (t == 0) | (te_ref[t] != te_ref[jnp.maximum(t - 1, 0)])

    @pl.when(new_expert & (t < na))
    def _():
        wgb[...] = wg_ref[0].astype(BF16)
        wub[...] = wu_ref[0].astype(BF16)
        wdb[...] = wd_ref[0].astype(BF16)

    @pl.when(t < na)
    def _():
        valid = vt_ref[t]

        def gather(i, c):
            for j in range(SUBLANES):
                r = i * SUBLANES + j
                row0 = pl.multiple_of(src_ref[0, 0, r] * rpt, rpt)
                xs[pl.ds(r, rpt, stride=row_stride), :] = hp_v[pl.ds(row0, rpt), :]
            return c
        lax.fori_loop(0, lax.shift_right_logical(valid + (SUBLANES - 1), 3), gather, 0)

        lo, hi = [], []
        for c in range(rpt):
            a, b = _unpack_pairs(xs[c * row_stride:c * row_stride + tm, :])
            lo.append(a.astype(BF16))
            hi.append(b.astype(BF16))
        x_lo = jnp.concatenate(lo, axis=-1)
        x_hi = jnp.concatenate(hi, axis=-1)
        g = _dot(x_lo, wgb[0:d_half, :]) + _dot(x_hi, wgb[d_half:, :])
        u = _dot(x_lo, wub[0:d_half, :]) + _dot(x_hi, wub[d_half:, :])
        act = ((g * jax.nn.sigmoid(g)) * u).astype(BF16)
        y = _dot(act, wdb[...])
        for c in range(2 * rpt):
            ys[c * row_stride:c * row_stride + tm, :] = y[:, c * LANES:(c + 1) * LANES]

        def accumulate(i, c):
            pending = []
            for j in range(MOE_RMW_BATCH):
                r = i * MOE_RMW_BATCH + j
                row0 = pl.multiple_of(src_ref[0, 0, r] * rpt, rpt)
                pending.append((row0, ws_ref[0, 0, r], acc_v[pl.ds(row0, rpt), :],
                                ys[pl.ds(r, rpt, stride=row_stride), :],
                                ys[pl.ds(rpt * row_stride + r, rpt, stride=row_stride), :]))
            for row0, w, words, y_lo, y_hi in pending:
                a_lo, a_hi = _unpack_pairs(words)
                acc_v[pl.ds(row0, rpt), :] = _pack_pairs(a_lo + w * y_lo, a_hi + w * y_hi)
            return c
        lax.fori_loop(0, lax.shift_right_logical(valid + (MOE_RMW_BATCH - 1), 2), accumulate, 0)

    @pl.when(t == pl.num_programs(0) - 1)
    def _():
        store = pltpu.make_async_copy(acc_v.at[pl.ds(0, tok_rows), :], m_hbm, sem.at[1])
        store.start()
        store.wait()


def _moe_half(tile_e, valid, n_active, src, ws, hp, w_gate, w_up, w_down, *, layer, half, n_half, tm):
    _, n_exp, d, f = w_gate.shape
    rpt = d // 2 // LANES
    n_tiles = tile_e.shape[0]
    row_stride = tm + SUBLANES
    smem_row = pl.BlockSpec((1, 1, tm), lambda t, te, vt, na: (t, 0, 0), memory_space=pltpu.SMEM)
    wspec = lambda a, b: pl.BlockSpec((None, 1, a, b), lambda t, te, vt, na: (layer, te[t], 0, 0))
    return pl.pallas_call(
        functools.partial(_moe_kernel, tm=tm, row_stride=row_stride, rpt=rpt, n_half=n_half, half=half),
        grid_spec=pltpu.PrefetchScalarGridSpec(
            num_scalar_prefetch=3,
            grid=(n_tiles,),
            in_specs=[smem_row, smem_row, pl.BlockSpec(memory_space=pl.ANY), wspec(d, f), wspec(d, f), wspec(f, d)],
            out_specs=pl.BlockSpec(memory_space=pl.ANY),
            scratch_shapes=[
                pltpu.VMEM(((n_half + 1) * rpt, LANES), jnp.uint32),
                pltpu.VMEM(((n_half + 1) * rpt, LANES), jnp.uint32),
                pltpu.VMEM((rpt * row_stride, LANES), jnp.uint32),
                pltpu.VMEM((2 * rpt * row_stride, LANES), F32),
                pltpu.VMEM((d, f), BF16), pltpu.VMEM((d, f), BF16), pltpu.VMEM((f, d), BF16),
                pltpu.SemaphoreType.DMA((2,)),
            ],
        ),
        out_shape=jax.ShapeDtypeStruct((n_half * rpt, LANES), jnp.uint32),
        compiler_params=_params(("arbitrary",)),
    )(tile_e, valid, n_active, src, ws, hp, w_gate, w_up, w_down)


def _dispatch_plan(ri, rw, cnt, *, n_groups, n_exp, tm, n_half):
    n = ri.shape[0]
    eid = ri[:, 0:EXPERT_TOP_K]
    rank = ri[:, EXPERT_TOP_K:2 * EXPERT_TOP_K]
    counts = cnt[0:2, n_groups:n_groups + n_exp].astype(jnp.int32)
    tiles_per = (counts + tm - 1) // tm
    tile_end = jnp.cumsum(tiles_per, axis=1)
    tile_start = tile_end - tiles_per
    n_active = tile_end[:, -1]
    n_tiles = (n_half * EXPERT_TOP_K) // tm + n_exp
    tok = jnp.arange(n, dtype=jnp.int32)
    half = (tok >= n_half).astype(jnp.int32)
    experts = jnp.arange(n_exp, dtype=jnp.int32)

    def lookup(table, idx):
        return jnp.sum(jnp.where(idx[..., None] == experts, table[..., None, :], 0), axis=-1)

    start_tok = lookup(jnp.where(half[:, None] == 0, tile_start[0][None, :], tile_start[1][None, :]), eid)
    slot = (half[:, None] * n_tiles + start_tok) * tm + rank
    payload = jnp.stack([jnp.broadcast_to((tok - half * n_half)[:, None], slot.shape),
                         lax.bitcast_convert_type(rw[:, 0:EXPERT_TOP_K], jnp.int32)], axis=-1)
    empty = jnp.broadcast_to(jnp.array([n_half, 0], jnp.int32), (2 * n_tiles * tm, 2))
    table = empty.at[slot.reshape(-1)].set(payload.reshape(-1, 2))
    src = table[:, 0]
    ws = lax.bitcast_convert_type(table[:, 1], F32)
    t_idx = jnp.arange(n_tiles, dtype=jnp.int32)[None, :]
    t_act = jnp.minimum(t_idx, n_active[:, None] - 1)
    tile_e = jnp.minimum(jnp.sum(t_act[:, :, None] >= tile_end[:, None, :], axis=-1), n_exp - 1).astype(jnp.int32)
    cnt_t = lookup(counts, tile_e)
    start_t = lookup(tile_start, tile_e)
    valid = jnp.where(t_idx < n_active[:, None], jnp.clip(cnt_t - (t_idx - start_t) * tm, 0, tm), 0)
    return (tile_e, valid.astype(jnp.int32), n_active.astype(jnp.int32),
            src.reshape(2, n_tiles, 1, tm), ws.reshape(2, n_tiles, 1, tm))


def _post_kernel(m0_ref, m1_ref, h_ref, p_ref, wp_ref, wpg_ref, bpg_ref,
                 g2_ref, b2_ref, g3_ref, b3_ref, o_ref, *, alpha):
    tm, d = h_ref.shape
    rpt = d // 2 // LANES
    first_half = pl.program_id(0) < pl.num_programs(0) // 2
    lo, hi = [], []
    for c in range(rpt):
        rows = pl.ds(c, tm, stride=rpt)
        a, b = _unpack_pairs(jnp.where(first_half, m0_ref[rows, :], m1_ref[rows, :]))
        lo.append(a)
        hi.append(b)
    m = jnp.concatenate(lo + hi, axis=-1)
    h2 = _layer_norm(alpha * h_ref[...] + m, g2_ref[...], b2_ref[...])
    gate = jax.nn.sigmoid(_dot(h2.astype(BF16), wpg_ref[...]) + bpg_ref[...])
    e = _dot(p_ref[...].astype(BF16), wp_ref[...]) * gate
    o_ref[...] = _layer_norm(alpha * h2 + e, g3_ref[...], b3_ref[...])


def _post(m0, m1, h, p3d, wp, wpg, bpg, g2, b2, g3, b3, *, layer, alpha, tm):
    n, d = h.shape
    ple = p3d.shape[2]
    rpt = d // 2 // LANES
    steps_half = n // tm // 2
    row = lambda w: pl.BlockSpec((tm, w), lambda i: (i, 0))
    return pl.pallas_call(
        functools.partial(_post_kernel, alpha=alpha),
        grid=(n // tm,),
        in_specs=[pl.BlockSpec((tm * rpt, LANES), lambda i: (jnp.minimum(i, steps_half - 1), 0)),
                  pl.BlockSpec((tm * rpt, LANES), lambda i: (jnp.maximum(i - steps_half, 0), 0)),
                  row(d), pl.BlockSpec((None, tm, ple), lambda i: (layer, i, 0)),
                  _const_spec(wp.shape), _const_spec(wpg.shape), _const_spec(bpg.shape),
                  _const_spec(g2.shape), _const_spec(b2.shape), _const_spec(g3.shape), _const_spec(b3.shape)],
        out_specs=row(d),
        out_shape=jax.ShapeDtypeStruct((n, d), F32),
        compiler_params=_params(("parallel",)),
    )(m0, m1, h, p3d, wp, wpg, bpg, g2, b2, g3, b3)


def _tiles(n, seq, ssm_w):
    block_ch = SETS_PER_STEP * GROUPS_PER_SET * CH_PER_GROUP
    return dict(
        inproj_tm=min(512, n),
        s5_tc=min(256, seq),
        s5_nblk=2 if ssm_w % (2 * block_ch) == 0 else 1,
        conv_tt=min(128, seq),
        mix_tm=min(512, n // 2),
        moe_tm=min(256, n),
        post_tm=min(512, n // 2),
    )


def kernel(x, p, w_in, b_in, lam_re, lam_im, log_dt, ssm_b_re, ssm_b_im, ssm_c_re, ssm_c_im, ssm_d, w_glu, b_glu, w_dw, b_dw, conv_ln_g, conv_ln_b, w_o, b_o, ln1_g, ln1_b, w_rg, b_rg, w_re, b_re, w_gate, w_up, w_down, ln2_g, ln2_b, w_p, w_pg, b_pg, ln3_g, ln3_b):
    depth = w_in.shape[0]
    nb, seq, d = x.shape
    n = nb * seq
    ssm_w = w_glu.shape[1]
    conv_w = w_dw.shape[2]
    n_groups = w_rg.shape[2]
    n_exp = w_re.shape[2]
    per_group = n_exp // n_groups
    alpha = (2 * depth) ** 0.25
    tl = _tiles(n, seq, ssm_w)
    row2 = lambda v: v.reshape(1, -1)

    x2d = x.reshape(n, d)
    p3d = p.reshape(depth, n, p.shape[-1])
    for i in range(depth):
        u, hc = _inproj(x2d, w_in[i].astype(BF16), row2(b_in[i]), ssm_w=ssm_w, conv_w=conv_w, tm=tl['inproj_tm'])

        bm, cm, ar, ai = _s5_params(lam_re[i], lam_im[i], log_dt[i], ssm_b_re[i], ssm_b_im[i],
                                    ssm_c_re[i], ssm_c_im[i], nb)
        z = _s5(u.reshape(nb, seq, ssm_w), bm, cm, ar, ai, row2(ssm_d[i]), tc=tl['s5_tc'],
                nblk=tl['s5_nblk']).reshape(n, ssm_w)

        yc = _conv(hc.reshape(nb, seq, conv_w), w_dw[i], row2(b_dw[i]), row2(conv_ln_g[i]), row2(conv_ln_b[i]),
                   tt=tl['conv_tt']).reshape(n, conv_w)

        wr = jnp.concatenate([w_rg[i], w_re[i]], axis=1)
        wr = jnp.pad(wr, ((0, 0), (0, LANES - wr.shape[1])))
        wrh = wr.astype(BF16)
        wrl = (wr - wrh.astype(F32)).astype(BF16)
        br = jnp.pad(jnp.concatenate([b_rg[i], b_re[i]]), (0, LANES - n_groups - n_exp)).reshape(1, LANES)
        h, hp, ri, rw, cnt = _mix_out(z, yc, x2d, w_glu[i].astype(BF16), row2(b_glu[i]), w_o[i].astype(BF16),
                                      row2(b_o[i]), row2(ln1_g[i]), row2(ln1_b[i]), wrh, wrl, br,
                                      alpha=alpha, n_groups=n_groups, per_group=per_group, tm=tl['mix_tm'])

        n_half = n // 2
        tile_e, valid, n_active, src, ws = _dispatch_plan(ri, rw, cnt, n_groups=n_groups, n_exp=n_exp,
                                                          tm=tl['moe_tm'], n_half=n_half)
        mix = [_moe_half(tile_e[k], valid[k], n_active[k:k + 1], src[k], ws[k], hp, w_gate, w_up, w_down,
                         layer=i, half=k, n_half=n_half, tm=tl['moe_tm']) for k in range(2)]

        x2d = _post(mix[0], mix[1], h, p3d, w_p[i].astype(BF16), w_pg[i].astype(BF16), row2(b_pg[i]),
                    row2(ln2_g[i]), row2(ln2_b[i]), row2(ln3_g[i]), row2(ln3_b[i]),
                    layer=i, alpha=alpha, tm=tl['post_tm'])
    return x2d.reshape(nb, seq, d)
```

```python
import functools
import math

import jax
import jax.numpy as jnp
from jax import lax
from jax.experimental import pallas as pl
from jax.experimental.pallas import tpu as pltpu

BF16 = jnp.bfloat16
F32 = jnp.float32

CH_PER_GROUP = 16
EXPERT_TOP_K = 2
LN_EPS = 1e-5
LAMBDA_RE_MAX = -1e-4

LANES = 128
SUBLANES = 8
V7X_VMEM_BYTES = 64 * 1024 * 1024
VMEM_LIMIT_BYTES = V7X_VMEM_BYTES - 8 * 1024 * 1024

GROUPS_PER_SET = 8
SETS_PER_STEP = 2


def _params(semantics):
    return pltpu.CompilerParams(dimension_semantics=semantics, vmem_limit_bytes=VMEM_LIMIT_BYTES)


def _const_spec(shape):
    nd = len(shape)
    return pl.BlockSpec(shape, lambda *_: (0,) * nd, pipeline_mode=pl.Buffered(1))


def _layer_spec(stacked_shape, layer):
    nd = len(stacked_shape) - 1
    return pl.BlockSpec((None,) + tuple(stacked_shape[1:]), lambda *_: (layer,) + (0,) * nd,
                        pipeline_mode=pl.Buffered(1))


def _layer_norm(v, g, b):
    mu = jnp.mean(v, axis=-1, keepdims=True)
    vc = v - mu
    var = jnp.mean(vc * vc, axis=-1, keepdims=True)
    return vc * lax.rsqrt(var + LN_EPS) * g + b


def _dot(a, b):
    return jnp.dot(a, b, preferred_element_type=F32)


def _inproj_kernel(x_ref, w_ref, b_ref, u_ref, hc_ref, *, ssm_w, conv_w):
    xb = x_ref[...].astype(BF16)

    def proj(lo, width):
        return _dot(xb, w_ref[:, lo:lo + width]) + b_ref[:, lo:lo + width]

    u_ref[...] = proj(0, ssm_w)
    v = proj(ssm_w, conv_w)
    g = proj(ssm_w + conv_w, conv_w)
    hc_ref[...] = v * jax.nn.sigmoid(g)


def _inproj(x2d, w_bf, b, *, layer, ssm_w, conv_w, tm):
    n, d = x2d.shape
    in_w = w_bf.shape[2]
    return pl.pallas_call(
        functools.partial(_inproj_kernel, ssm_w=ssm_w, conv_w=conv_w),
        grid=(n // tm,),
        in_specs=[pl.BlockSpec((tm, d), lambda i: (i, 0)), _layer_spec(w_bf.shape, layer), _const_spec((1, in_w))],
        out_specs=[pl.BlockSpec((tm, ssm_w), lambda i: (i, 0)), pl.BlockSpec((tm, conv_w), lambda i: (i, 0))],
        out_shape=[jax.ShapeDtypeStruct((n, ssm_w), F32), jax.ShapeDtypeStruct((n, conv_w), F32)],
        compiler_params=_params(("parallel",)),
    )(x2d, w_bf, b)


def _gelu_tanh(y):
    c = math.sqrt(2.0 / math.pi)
    return y * (0.5 * (1.0 + jnp.tanh(c * (y + 0.044715 * (y * y * y)))))


def _s5_kernel(u_ref, bm_ref, cm_ref, ar_ref, ai_ref, d_ref, z_ref, lhs_ref, s_ref, y_ref, xr_ref, xi_ref,
               *, tc, unroll, nblk, mm_rows):
    nb = u_ref.shape[0]
    set_ch = GROUPS_PER_SET * CH_PER_GROUP
    assert set_ch == LANES
    n_lt = s_ref.shape[0] // nblk
    n_c = n_lt // 2
    rows = SETS_PER_STEP * nb
    lt = lambda j: slice(j * LANES, (j + 1) * LANES)
    ch = lambda q, s: slice((q * SETS_PER_STEP + s) * set_ch, (q * SETS_PER_STEP + s + 1) * set_ch)
    slabs = [(s, b) for s in range(SETS_PER_STEP) for b in range(nb)]
    chunks = range(0, tc * rows, mm_rows)

    @pl.when(pl.program_id(1) == 0)
    def _():
        xr_ref[...] = jnp.zeros_like(xr_ref)
        xi_ref[...] = jnp.zeros_like(xi_ref)
        lhs_ref[...] = jnp.zeros_like(lhs_ref)

    slab_rows = lambda s, b: pl.ds(s * nb + b, tc, stride=rows)

    for q in range(nblk):
        for s, b in slabs:
            lhs_ref[q * SETS_PER_STEP + s, slab_rows(s, b), :] = u_ref[b, :, ch(q, s)]
        for r0 in chunks:
            lhs = jnp.concatenate([lhs_ref[q * SETS_PER_STEP + s, r0:r0 + mm_rows, :]
                                   for s in range(SETS_PER_STEP)], axis=-1).astype(BF16)
            bu = _dot(lhs, bm_ref[q])
            for j in range(n_lt):
                s_ref[q * n_lt + j, r0:r0 + mm_rows, :] = bu[:, lt(j)]

    chains = [(q, j) for q in range(nblk) for j in range(n_c)]

    def step(t, carry):
        rows_t = pl.ds(pl.multiple_of(t * rows, rows), rows)
        out = []
        for (q, j), (xr, xi) in zip(chains, carry):
            ar = ar_ref[q, :, lt(j)]
            ai = ai_ref[q, :, lt(j)]
            nxr = ar * xr - ai * xi + s_ref[q * n_lt + j, rows_t, :]
            nxi = ar * xi + ai * xr + s_ref[q * n_lt + n_c + j, rows_t, :]
            s_ref[q * n_lt + j, rows_t, :] = nxr
            s_ref[q * n_lt + n_c + j, rows_t, :] = nxi
            out.append((nxr, nxi))
        return tuple(out)

    def block(i, carry):
        for j in range(unroll):
            carry = step(i * unroll + j, carry)
        return carry

    init = tuple((xr_ref[q, :, lt(j)], xi_ref[q, :, lt(j)]) for q, j in chains)
    final = lax.fori_loop(0, tc // unroll, block, init)
    for (q, j), (xr, xi) in zip(chains, final):
        xr_ref[q, :, lt(j)] = xr
        xi_ref[q, :, lt(j)] = xi

    for q in range(nblk):
        for r0 in chunks:
            st = jnp.concatenate([s_ref[q * n_lt + j, r0:r0 + mm_rows, :].astype(BF16) for j in range(n_lt)],
                                 axis=-1)
            yy = _dot(st, cm_ref[q])
            for s in range(SETS_PER_STEP):
                y_ref[s, r0:r0 + mm_rows, :] = yy[:, lt(s)]
        for s, b in slabs:
            y = y_ref[s, slab_rows(s, b), :] + d_ref[:, ch(q, s)] * u_ref[b, :, ch(q, s)]
            z_ref[b, :, ch(q, s)] = _gelu_tanh(y)


def _s5(u3d, bm, cm, ar, ai, d, *, tc, nblk):
    nb, seq, ssm_w = u3d.shape
    step_ch = nblk * SETS_PER_STEP * GROUPS_PER_SET * CH_PER_GROUP
    state_w = bm.shape[-1]
    n_lt = state_w // LANES
    assert SETS_PER_STEP * nb == SUBLANES, "the scan fills the 8 sublanes with (set, batch)"
    tr = SUBLANES * tc
    lam_spec = pl.BlockSpec((nblk, SUBLANES, state_w // 2), lambda g, t: (g, 0, 0))
    return pl.pallas_call(
        functools.partial(_s5_kernel, tc=tc, unroll=8, nblk=nblk, mm_rows=min(512, tr)),
        grid=(ssm_w // step_ch, seq // tc),
        in_specs=[
            pl.BlockSpec((nb, tc, step_ch), lambda g, t: (0, t, g)),
            pl.BlockSpec((nblk, bm.shape[1], state_w), lambda g, t: (g, 0, 0)),
            pl.BlockSpec((nblk, state_w, cm.shape[2]), lambda g, t: (g, 0, 0)),
            lam_spec, lam_spec,
            pl.BlockSpec((1, step_ch), lambda g, t: (0, g)),
        ],
        out_specs=pl.BlockSpec((nb, tc, step_ch), lambda g, t: (0, t, g)),
        out_shape=jax.ShapeDtypeStruct((nb, seq, ssm_w), F32),
        scratch_shapes=[
            pltpu.VMEM((nblk * SETS_PER_STEP, tr, LANES), F32),
            pltpu.VMEM((nblk * n_lt, tr, LANES), F32),
            pltpu.VMEM((SETS_PER_STEP, tr, LANES), F32),
            pltpu.VMEM((nblk, SUBLANES, state_w // 2), F32),
            pltpu.VMEM((nblk, SUBLANES, state_w // 2), F32),
        ],
        compiler_params=_params(("parallel", "arbitrary")),
    )(u3d, bm, cm, ar, ai, d)


def _s5_params(lam_re, lam_im, log_dt, b_re, b_im, c_re, c_im, nb):
    g, p = lam_re.shape
    lam = lax.complex(jnp.minimum(lam_re, LAMBDA_RE_MAX), lam_im)
    dt = jnp.exp(log_dt)[:, None]
    lam_bar = jnp.exp(lam * dt)
    b_bar = ((lam_bar - 1.0) / lam)[..., None] * lax.complex(b_re, b_im)
    n_set = g // GROUPS_PER_SET
    eye = jnp.eye(GROUPS_PER_SET, dtype=F32)

    def in_mat(part):
        blk = part.reshape(n_set, GROUPS_PER_SET, p, CH_PER_GROUP)
        m = jnp.einsum('sgph,gk->sghkp', blk, eye)
        return m.reshape(n_set, GROUPS_PER_SET * CH_PER_GROUP, GROUPS_PER_SET * p)

    def out_mat(part):
        blk = part.reshape(n_set, GROUPS_PER_SET, CH_PER_GROUP, p)
        m = jnp.einsum('sghp,gk->sgpkh', blk, eye)
        return m.reshape(n_set, GROUPS_PER_SET * p, GROUPS_PER_SET * CH_PER_GROUP)

    bm = jnp.concatenate([in_mat(jnp.real(b_bar)), in_mat(jnp.imag(b_bar))], axis=-1).astype(BF16)
    cm = jnp.concatenate([out_mat(c_re), out_mat(-c_im)], axis=1).astype(BF16)
    n_blk = n_set // SETS_PER_STEP
    bm = bm.reshape(n_blk, SETS_PER_STEP * bm.shape[1], bm.shape[2])
    cm = cm.reshape(n_blk, SETS_PER_STEP, cm.shape[1], cm.shape[2]).transpose(0, 2, 1, 3)
    cm = cm.reshape(n_blk, cm.shape[1], -1)

    def lam_rows(part):
        v = part.reshape(n_set // SETS_PER_STEP, SETS_PER_STEP, 1, GROUPS_PER_SET * p)
        v = jnp.broadcast_to(v, (n_set // SETS_PER_STEP, SETS_PER_STEP, nb, GROUPS_PER_SET * p))
        return v.reshape(n_set // SETS_PER_STEP, SETS_PER_STEP * nb, GROUPS_PER_SET * p)

    return bm, cm, lam_rows(jnp.real(lam_bar)), lam_rows(jnp.imag(lam_bar))


CONV_HALO = 32
CONV_ROWS = 64
CONV_LANES = 256


def _conv_kernel(cur_ref, halo_ref, w_ref, b_ref, g_ref, beta_ref, o_ref, buf_ref, *, tt, taps):
    halo = halo_ref[0]
    width = cur_ref.shape[-1]
    buf_ref[0:CONV_HALO, :] = jnp.where(pl.program_id(1) == 0, jnp.zeros_like(halo), halo)
    buf_ref[CONV_HALO:CONV_HALO + tt, :] = cur_ref[0]
    buf_ref[CONV_HALO + tt:, :] = jnp.zeros((SUBLANES, width), F32)
    lead = CONV_HALO - (taps - 1)
    for c0 in range(0, width, CONV_LANES):
        cs = slice(c0, c0 + CONV_LANES)
        for r0 in range(0, tt, CONV_ROWS):
            acc = jnp.broadcast_to(b_ref[:, cs], (CONV_ROWS, CONV_LANES))
            for s in range(SUBLANES):
                part = None
                for off in range(s, lead + taps, SUBLANES):
                    k = off - lead
                    if k < 0:
                        continue
                    term = buf_ref[r0 + off - s:r0 + off - s + CONV_ROWS + SUBLANES, cs] * w_ref[k:k + 1, cs]
                    part = term if part is None else part + term
                acc = acc + part[s:s + CONV_ROWS]
            o_ref[0, r0:r0 + CONV_ROWS, cs] = acc
    h = _layer_norm(o_ref[0], g_ref[...], beta_ref[...])
    o_ref[0] = h * jax.nn.sigmoid(h)


def _conv(hc3d, w_dw, b_dw, ln_g, ln_b, *, tt):
    nb, seq, width = hc3d.shape
    taps = w_dw.shape[0]
    assert taps - 1 <= CONV_HALO and tt % CONV_HALO == 0
    per = tt // CONV_HALO
    return pl.pallas_call(
        functools.partial(_conv_kernel, tt=tt, taps=taps),
        grid=(nb, seq // tt),
        in_specs=[
            pl.BlockSpec((1, tt, width), lambda b, t: (b, t, 0)),
            pl.BlockSpec((1, CONV_HALO, width), lambda b, t: (b, jnp.maximum(t * per - 1, 0), 0)),
            _const_spec((taps, width)), _const_spec((1, width)), _const_spec((1, width)), _const_spec((1, width)),
        ],
        out_specs=pl.BlockSpec((1, tt, width), lambda b, t: (b, t, 0)),
        out_shape=jax.ShapeDtypeStruct((nb, seq, width), F32),
        scratch_shapes=[pltpu.VMEM((CONV_HALO + tt + SUBLANES, width), F32)],
        compiler_params=_params(("parallel", "parallel")),
    )(hc3d, hc3d, w_dw, b_dw, ln_g, ln_b)


ROW_SPLIT = 2


def _first_lane_of_max(vals, lane):
    m = jnp.max(vals, axis=-1, keepdims=True)
    idx = jnp.min(jnp.where(vals == m, lane, LANES), axis=-1, keepdims=True)
    return m, idx


def _pack_pairs(lo, hi):
    lo_bits = lax.bitcast_convert_type(lo.astype(BF16).astype(F32), jnp.uint32)
    hi_bits = lax.bitcast_convert_type(hi.astype(BF16).astype(F32), jnp.uint32)
    return hi_bits | (lo_bits >> 16)


def _unpack_pairs(words):
    lo = lax.bitcast_convert_type(words << 16, F32)
    hi = lax.bitcast_convert_type(words & jnp.uint32(0xFFFF0000), F32)
    return lo, hi


def _mix_out_kernel(z_ref, yc_ref, x_ref, wglu_ref, bglu_ref, wo_ref, bo_ref, g1_ref, b1_ref,
                    wrh_ref, wrl_ref, br_ref, h_ref, hp_ref, ri_ref, rw_ref, cnt_ref, carry_ref,
                    *, alpha, ssm_w, n_groups, per_group):
    tm = z_ref.shape[0]
    d = x_ref.shape[1]
    rpt = d // 2 // LANES
    i = pl.program_id(0)
    steps_half = pl.num_programs(0) // 2

    @pl.when(i == 0)
    def _():
        cnt_ref[...] = jnp.zeros_like(cnt_ref)

    @pl.when((i == 0) | (i == steps_half))
    def _():
        carry_ref[...] = jnp.zeros_like(carry_ref)

    sub = tm // ROW_SPLIT
    logit_parts = []
    for k in range(ROW_SPLIT):
        rs = slice(k * sub, (k + 1) * sub)
        z = z_ref[rs, :]
        zs = z * jax.nn.sigmoid(_dot(z.astype(BF16), wglu_ref[...]) + bglu_ref[...])
        y = (_dot(zs.astype(BF16), wo_ref[0:ssm_w, :]) + _dot(yc_ref[rs, :].astype(BF16), wo_ref[ssm_w:, :])
             + bo_ref[...])
        h = _layer_norm(alpha * x_ref[rs, :] + y, g1_ref[...], b1_ref[...])
        h_ref[rs, :] = h
        for c in range(rpt):
            lo = h[:, c * LANES:(c + 1) * LANES]
            hi = h[:, d // 2 + c * LANES:d // 2 + (c + 1) * LANES]
            hp_ref[pl.ds(k * sub * rpt + c, sub, stride=rpt), :] = _pack_pairs(lo, hi)
        hh = h.astype(BF16)
        hl = (h - hh.astype(F32)).astype(BF16)
        logit_parts.append(_dot(hh, wrh_ref[...]) + (_dot(hh, wrl_ref[...]) + _dot(hl, wrh_ref[...])) + br_ref[...])
    logits = jnp.concatenate(logit_parts, axis=0)

    lane = lax.broadcasted_iota(jnp.int32, (tm, LANES), 1)
    neg = -jnp.inf
    gmask = lane < n_groups
    gmax, gidx = _first_lane_of_max(jnp.where(gmask, logits, neg), lane)
    gsum = jnp.sum(jnp.where(gmask, jnp.exp(logits - gmax), 0.0), axis=-1, keepdims=True)
    g_p = 1.0 / gsum
    elo = n_groups + gidx * per_group
    le = jnp.where((lane >= elo) & (lane < elo + per_group), logits, neg)
    v1, i1 = _first_lane_of_max(le, lane)
    v2, i2 = _first_lane_of_max(jnp.where(lane == i1, neg, le), lane)
    e2 = jnp.exp(v2 - v1)
    w1 = g_p / (1.0 + e2)
    w2 = w1 * e2

    oh1 = lane == i1
    oh2 = lane == i2
    oh = jnp.where(oh1 | oh2, 1.0, 0.0)
    row = lax.broadcasted_iota(jnp.int32, (tm, tm), 0)
    col = lax.broadcasted_iota(jnp.int32, (tm, tm), 1)
    earlier = jnp.where(col < row, 1.0, 0.0).astype(BF16)
    base = _dot(earlier, oh.astype(BF16)) + carry_ref[...]
    r1 = jnp.sum(jnp.where(oh1, base, 0.0), axis=-1, keepdims=True).astype(jnp.int32)
    r2 = jnp.sum(jnp.where(oh2, base, 0.0), axis=-1, keepdims=True).astype(jnp.int32)
    carry_ref[...] += jnp.sum(oh, axis=0, keepdims=True)

    @pl.when(i < steps_half)
    def _():
        cnt_ref[0:1, :] = carry_ref[...]

    @pl.when(i >= steps_half)
    def _():
        cnt_ref[1:2, :] = carry_ref[...]

    ri_ref[...] = jnp.where(lane == 0, i1 - n_groups,
                            jnp.where(lane == 1, i2 - n_groups,
                                      jnp.where(lane == 2, r1, jnp.where(lane == 3, r2, 0))))
    rw_ref[...] = jnp.where(lane == 0, w1, jnp.where(lane == 1, w2, 0.0))


def _mix_out(z, yc, x2d, wglu, bglu, wo, bo, g1, b1, wrh, wrl, br, *, layer, alpha, n_groups, per_group, tm):
    n, d = x2d.shape
    ssm_w = z.shape[1]
    conv_w = yc.shape[1]
    rpt = d // 2 // LANES
    assert (n // tm) % 2 == 0 and tm % ROW_SPLIT == 0
    row = lambda w: pl.BlockSpec((tm, w), lambda i: (i, 0))
    return pl.pallas_call(
        functools.partial(_mix_out_kernel, alpha=alpha, ssm_w=ssm_w, n_groups=n_groups, per_group=per_group),
        grid=(n // tm,),
        in_specs=[row(ssm_w), row(conv_w), row(d),
                  _layer_spec(wglu.shape, layer), _const_spec(bglu.shape), _layer_spec(wo.shape, layer),
                  _const_spec(bo.shape),
                  _const_spec(g1.shape), _const_spec(b1.shape),
                  _const_spec(wrh.shape), _const_spec(wrl.shape), _const_spec(br.shape)],
        out_specs=[row(d), pl.BlockSpec((tm * rpt, LANES), lambda i: (i, 0)), row(LANES), row(LANES),
                   _const_spec((SUBLANES, LANES))],
        out_shape=[jax.ShapeDtypeStruct((n, d), F32), jax.ShapeDtypeStruct((n * rpt, LANES), jnp.uint32),
                   jax.ShapeDtypeStruct((n, LANES), jnp.int32), jax.ShapeDtypeStruct((n, LANES), F32),
                   jax.ShapeDtypeStruct((SUBLANES, LANES), F32)],
        scratch_shapes=[pltpu.VMEM((1, LANES), F32)],
        compiler_params=_params(("arbitrary",)),
    )(z, yc, x2d, wglu, bglu, wo, bo, g1, b1, wrh, wrl, br)


MOE_RMW_BATCH = 8


def _moe_kernel(te_ref, vt_ref, na_ref, src_ref, ws_ref, hp_hbm, wg_ref, wu_ref, wd_ref, m_hbm,
                hp_v, acc_v, xs, ys, wgb, wub, wdb, sem, *, tm, row_stride, rpt, n_half, half):
    t = pl.program_id(0)
    na = na_ref[0]
    d_half = rpt * LANES
    tok_rows = n_half * rpt

    @pl.when(t == 0)
    def _():
        load = pltpu.make_async_copy(hp_hbm.at[pl.ds(half * tok_rows, tok_rows), :],
                                     hp_v.at[pl.ds(0, tok_rows), :], sem.at[0])
        load.start()
        acc_v[...] = jnp.zeros_like(acc_v)
        xs[...] = jnp.zeros_like(xs)
        hp_v[tok_rows:tok_rows + rpt, :] = jnp.zeros((rpt, LANES), jnp.uint32)
        load.wait()

    new_expert = (t == 0) | (te_ref[t] != te_ref[jnp.maximum(t - 1, 0)])

    @pl.when(new_expert & (t < na))
    def _():
        wgb[...] = wg_ref[0].astype(BF16)
        wub[...] = wu_ref[0].astype(BF16)
        wdb[...] = wd_ref[0].astype(BF16)

    @pl.when(t < na)
    def _():
        valid = vt_ref[t]

        def gather(i, c):
            for j in range(SUBLANES):
                r = i * SUBLANES + j
                row0 = pl.multiple_of(src_ref[0, 0, r] * rpt, rpt)
                xs[pl.ds(r, rpt, stride=row_stride), :] = hp_v[pl.ds(row0, rpt), :]
            return c
        lax.fori_loop(0, lax.shift_right_logical(valid + (SUBLANES - 1), SUBLANES.bit_length() - 1), gather, 0)

        lo, hi = [], []
        for c in range(rpt):
            a, b = _unpack_pairs(xs[c * row_stride:c * row_stride + tm, :])
            lo.append(a.astype(BF16))
            hi.append(b.astype(BF16))
        x_lo = jnp.concatenate(lo, axis=-1)
        x_hi = jnp.concatenate(hi, axis=-1)
        g = _dot(x_lo, wgb[0:d_half, :]) + _dot(x_hi, wgb[d_half:, :])
        u = _dot(x_lo, wub[0:d_half, :]) + _dot(x_hi, wub[d_half:, :])
        act = ((g * jax.nn.sigmoid(g)) * u).astype(BF16)
        y = _dot(act, wdb[...])
        for c in range(2 * rpt):
            ys[c * row_stride:c * row_stride + tm, :] = y[:, c * LANES:(c + 1) * LANES]

        def accumulate(i, c):
            pending = []
            for j in range(MOE_RMW_BATCH):
                r = i * MOE_RMW_BATCH + j
                row0 = pl.multiple_of(src_ref[0, 0, r] * rpt, rpt)
                pending.append((row0, ws_ref[0, 0, r], acc_v[pl.ds(row0, rpt), :],
                                ys[pl.ds(r, rpt, stride=row_stride), :],
                                ys[pl.ds(rpt * row_stride + r, rpt, stride=row_stride), :]))
            for row0, w, words, y_lo, y_hi in pending:
                a_lo, a_hi = _unpack_pairs(words)
                acc_v[pl.ds(row0, rpt), :] = _pack_pairs(a_lo + w * y_lo, a_hi + w * y_hi)
            return c
        lax.fori_loop(0, lax.shift_right_logical(valid + (MOE_RMW_BATCH - 1), MOE_RMW_BATCH.bit_length() - 1),
                      accumulate, 0)

    @pl.when(t == pl.num_programs(0) - 1)
    def _():
        store = pltpu.make_async_copy(acc_v.at[pl.ds(0, tok_rows), :], m_hbm, sem.at[1])
        store.start()
        store.wait()


def _moe_half(tile_e, valid, n_active, src, ws, hp, w_gate, w_up, w_down, *, layer, half, n_half, tm):
    _, n_exp, d, f = w_gate.shape
    rpt = d // 2 // LANES
    n_tiles = tile_e.shape[0]
    row_stride = tm + SUBLANES
    smem_row = pl.BlockSpec((1, 1, tm), lambda t, te, vt, na: (t, 0, 0), memory_space=pltpu.SMEM)
    wspec = lambda a, b: pl.BlockSpec((None, 1, a, b), lambda t, te, vt, na: (layer, te[t], 0, 0))
    return pl.pallas_call(
        functools.partial(_moe_kernel, tm=tm, row_stride=row_stride, rpt=rpt, n_half=n_half, half=half),
        grid_spec=pltpu.PrefetchScalarGridSpec(
            num_scalar_prefetch=3,
            grid=(n_tiles,),
            in_specs=[smem_row, smem_row, pl.BlockSpec(memory_space=pl.ANY), wspec(d, f), wspec(d, f), wspec(f, d)],
            out_specs=pl.BlockSpec(memory_space=pl.ANY),
            scratch_shapes=[
                pltpu.VMEM(((n_half + 1) * rpt, LANES), jnp.uint32),
                pltpu.VMEM(((n_half + 1) * rpt, LANES), jnp.uint32),
                pltpu.VMEM((rpt * row_stride, LANES), jnp.uint32),
                pltpu.VMEM((2 * rpt * row_stride, LANES), F32),
                pltpu.VMEM((d, f), BF16), pltpu.VMEM((d, f), BF16), pltpu.VMEM((f, d), BF16),
                pltpu.SemaphoreType.DMA((2,)),
            ],
        ),
        out_shape=jax.ShapeDtypeStruct((n_half * rpt, LANES), jnp.uint32),
        compiler_params=_params(("arbitrary",)),
    )(tile_e, valid, n_active, src, ws, hp, w_gate, w_up, w_down)


def _dispatch_plan(ri, rw, cnt, *, n_groups, n_exp, tm, n_half):
    n = ri.shape[0]
    eid = ri[:, 0:EXPERT_TOP_K]
    rank = ri[:, EXPERT_TOP_K:2 * EXPERT_TOP_K]
    counts = cnt[0:2, n_groups:n_groups + n_exp].astype(jnp.int32)
    tiles_per = (counts + tm - 1) // tm
    tile_end = jnp.cumsum(tiles_per, axis=1)
    tile_start = tile_end - tiles_per
    n_active = tile_end[:, -1]
    n_tiles = (n_half * EXPERT_TOP_K) // tm + n_exp
    tok = jnp.arange(n, dtype=jnp.int32)
    half = (tok >= n_half).astype(jnp.int32)
    experts = jnp.arange(n_exp, dtype=jnp.int32)

    def lookup(table, idx):
        return jnp.sum(jnp.where(idx[..., None] == experts, table[..., None, :], 0), axis=-1)

    start_tok = lookup(jnp.where(half[:, None] == 0, tile_start[0][None, :], tile_start[1][None, :]), eid)
    slot = (half[:, None] * n_tiles + start_tok) * tm + rank
    payload = jnp.stack([jnp.broadcast_to((tok - half * n_half)[:, None], slot.shape),
                         lax.bitcast_convert_type(rw[:, 0:EXPERT_TOP_K], jnp.int32)], axis=-1)
    empty = jnp.broadcast_to(jnp.array([n_half, 0], jnp.int32), (2 * n_tiles * tm, 2))
    table = empty.at[slot.reshape(-1)].set(payload.reshape(-1, 2))
    src = table[:, 0]
    ws = lax.bitcast_convert_type(table[:, 1], F32)
    t_idx = jnp.arange(n_tiles, dtype=jnp.int32)[None, :]
    t_act = jnp.minimum(t_idx, n_active[:, None] - 1)
    tile_e = jnp.minimum(jnp.sum(t_act[:, :, None] >= tile_end[:, None, :], axis=-1), n_exp - 1).astype(jnp.int32)
    cnt_t = lookup(counts, tile_e)
    start_t = lookup(tile_start, tile_e)
    valid = jnp.where(t_idx < n_active[:, None], jnp.clip(cnt_t - (t_idx - start_t) * tm, 0, tm), 0)
    return (tile_e, valid.astype(jnp.int32), n_active.astype(jnp.int32),
            src.reshape(2, n_tiles, 1, tm), ws.reshape(2, n_tiles, 1, tm))


def _post_kernel(m0_ref, m1_ref, h_ref, p_ref, wp_ref, wpg_ref, bpg_ref,
                 g2_ref, b2_ref, g3_ref, b3_ref, o_ref, *, alpha):
    tm, d = h_ref.shape
    rpt = d // 2 // LANES
    first_half = pl.program_id(0) < pl.num_programs(0) // 2
    sub = tm // ROW_SPLIT
    for k in range(ROW_SPLIT):
        rs = slice(k * sub, (k + 1) * sub)
        lo, hi = [], []
        for c in range(rpt):
            rows = pl.ds(k * sub * rpt + c, sub, stride=rpt)
            a, b = _unpack_pairs(jnp.where(first_half, m0_ref[rows, :], m1_ref[rows, :]))
            lo.append(a)
            hi.append(b)
        m = jnp.concatenate(lo + hi, axis=-1)
        h2 = _layer_norm(alpha * h_ref[rs, :] + m, g2_ref[...], b2_ref[...])
        gate = jax.nn.sigmoid(_dot(h2.astype(BF16), wpg_ref[...]) + bpg_ref[...])
        e = _dot(p_ref[rs, :].astype(BF16), wp_ref[...]) * gate
        o_ref[rs, :] = _layer_norm(alpha * h2 + e, g3_ref[...], b3_ref[...])


def _post(m0, m1, h, p3d, wp, wpg, bpg, g2, b2, g3, b3, *, layer, alpha, tm):
    n, d = h.shape
    ple = p3d.shape[2]
    rpt = d // 2 // LANES
    steps_half = n // tm // 2
    assert tm % ROW_SPLIT == 0
    row = lambda w: pl.BlockSpec((tm, w), lambda i: (i, 0))
    return pl.pallas_call(
        functools.partial(_post_kernel, alpha=alpha),
        grid=(n // tm,),
        in_specs=[pl.BlockSpec((tm * rpt, LANES), lambda i: (jnp.minimum(i, steps_half - 1), 0)),
                  pl.BlockSpec((tm * rpt, LANES), lambda i: (jnp.maximum(i - steps_half, 0), 0)),
                  row(d), pl.BlockSpec((None, tm, ple), lambda i: (layer, i, 0)),
                  _layer_spec(wp.shape, layer), _layer_spec(wpg.shape, layer), _const_spec(bpg.shape),
                  _const_spec(g2.shape), _const_spec(b2.shape), _const_spec(g3.shape), _const_spec(b3.shape)],
        out_specs=row(d),
        out_shape=jax.ShapeDtypeStruct((n, d), F32),
        compiler_params=_params(("parallel",)),
    )(m0, m1, h, p3d, wp, wpg, bpg, g2, b2, g3, b3)


def _tiles(n, seq, ssm_w):
    block_ch = SETS_PER_STEP * GROUPS_PER_SET * CH_PER_GROUP
    return dict(
        inproj_tm=min(512, n),
        s5_tc=min(256, seq),
        s5_nblk=2 if ssm_w % (2 * block_ch) == 0 else 1,
        conv_tt=min(128, seq),
        mix_tm=min(512, n // 2),
        moe_tm=min(256, n),
        post_tm=min(512, n // 2),
    )


def kernel(x, p, w_in, b_in, lam_re, lam_im, log_dt, ssm_b_re, ssm_b_im, ssm_c_re, ssm_c_im, ssm_d, w_glu, b_glu, w_dw, b_dw, conv_ln_g, conv_ln_b, w_o, b_o, ln1_g, ln1_b, w_rg, b_rg, w_re, b_re, w_gate, w_up, w_down, ln2_g, ln2_b, w_p, w_pg, b_pg, ln3_g, ln3_b):
    depth = w_in.shape[0]
    nb, seq, d = x.shape
    n = nb * seq
    ssm_w = w_glu.shape[1]
    conv_w = w_dw.shape[2]
    n_groups = w_rg.shape[2]
    n_exp = w_re.shape[2]
    per_group = n_exp // n_groups
    alpha = (2 * depth) ** 0.25
    tl = _tiles(n, seq, ssm_w)
    row2 = lambda v: v.reshape(1, -1)

    x2d = x.reshape(n, d)
    p3d = p.reshape(depth, n, p.shape[-1])
    w_in_bf, w_glu_bf, w_o_bf, w_p_bf, w_pg_bf = (w.astype(BF16) for w in (w_in, w_glu, w_o, w_p, w_pg))
    for i in range(depth):
        u, hc = _inproj(x2d, w_in_bf, row2(b_in[i]), layer=i, ssm_w=ssm_w, conv_w=conv_w, tm=tl['inproj_tm'])

        bm, cm, ar, ai = _s5_params(lam_re[i], lam_im[i], log_dt[i], ssm_b_re[i], ssm_b_im[i],
                                    ssm_c_re[i], ssm_c_im[i], nb)
        z = _s5(u.reshape(nb, seq, ssm_w), bm, cm, ar, ai, row2(ssm_d[i]), tc=tl['s5_tc'],
                nblk=tl['s5_nblk']).reshape(n, ssm_w)

        yc = _conv(hc.reshape(nb, seq, conv_w), w_dw[i], row2(b_dw[i]), row2(conv_ln_g[i]), row2(conv_ln_b[i]),
                   tt=tl['conv_tt']).reshape(n, conv_w)

        wr = jnp.concatenate([w_rg[i], w_re[i]], axis=1)
        wr = jnp.pad(wr, ((0, 0), (0, LANES - wr.shape[1])))
        wrh = wr.astype(BF16)
        wrl = (wr - wrh.astype(F32)).astype(BF16)
        br = jnp.pad(jnp.concatenate([b_rg[i], b_re[i]]), (0, LANES - n_groups - n_exp)).reshape(1, LANES)
        h, hp, ri, rw, cnt = _mix_out(z, yc, x2d, w_glu_bf, row2(b_glu[i]), w_o_bf,
                                      row2(b_o[i]), row2(ln1_g[i]), row2(ln1_b[i]), wrh, wrl, br, layer=i,
                                      alpha=alpha, n_groups=n_groups, per_group=per_group, tm=tl['mix_tm'])

        n_half = n // 2
        tile_e, valid, n_active, src, ws = _dispatch_plan(ri, rw, cnt, n_groups=n_groups, n_exp=n_exp,
                                                          tm=tl['moe_tm'], n_half=n_half)
        mix = [_moe_half(tile_e[k], valid[k], n_active[k:k + 1], src[k], ws[k], hp, w_gate, w_up, w_down,
                         layer=i, half=k, n_half=n_half, tm=tl['moe_tm']) for k in range(2)]

        x2d = _post(mix[0], mix[1], h, p3d, w_p_bf, w_pg_bf, row2(b_pg[i]),
                    row2(ln2_g[i]), row2(ln2_b[i]), row2(ln3_g[i]), row2(ln3_b[i]),
                    layer=i, alpha=alpha, tm=tl['post_tm'])
    return x2d.reshape(nb, seq, d)
```
